```python
import jax, jax.numpy as jnp
from jax import lax
import numpy as np

D_MODEL = 1024
BATCH = 8
SEQ = 8192
DEPTH = 1
DEC_BATCH = 16
DEC_SEQ = 16
PAST_LEN = 1024

CHUNK = 64
MIX_WIDTH = D_MODEL
HEAD_DIM = 64
RWKV_WIDTH = MIX_WIDTH // 2
RWKV_HEADS = RWKV_WIDTH // HEAD_DIM
ATT_WIDTH = MIX_WIDTH - RWKV_WIDTH
ATT_HEADS = ATT_WIDTH // HEAD_DIM
DECAY_LORA = 64
ICLR_LORA = 64
GATE_LORA = 128
RWKV_COLS = 3 * RWKV_WIDTH + DECAY_LORA + ICLR_LORA + GATE_LORA
RWKV_SPLITS = [RWKV_WIDTH, 2 * RWKV_WIDTH, 3 * RWKV_WIDTH,
               3 * RWKV_WIDTH + DECAY_LORA, 3 * RWKV_WIDTH + DECAY_LORA + ICLR_LORA]
ATT_COLS = 3 * ATT_WIDTH
IN_COLS = RWKV_COLS + ATT_COLS
LEFT_CHUNKS = 8
ATT_WINDOW = LEFT_CHUNKS * CHUNK
BAND = ATT_WINDOW + CHUNK
REL_MAX = 128
N_REL = 2 * REL_MAX + 1
D_FF = 2816
NORM_EPS = 1e-5
GN_EPS = 64e-5
NEG_INF = -1e30

kernel_name = "rwkv7_chunkband_hymba_macaron_step"


def _rms_norm(x, g):
    xf = x.astype(jnp.float32)
    xf = xf * lax.rsqrt(jnp.mean(xf * xf, axis=-1, keepdims=True) + NORM_EPS)
    return (xf * g.astype(jnp.float32)).astype(x.dtype)


def _swiglu(x, w_gate, w_up, w_down):
    return (jax.nn.silu(x @ w_gate) * (x @ w_up)) @ w_down


def _wkv_scan(r, decay, k, v, kk, a, s0):
    def step(s, inp):
        r_t, w_t, k_t, v_t, kk_t, a_t = inp
        sa = jnp.einsum('bhvk,bhk->bhv', s, kk_t)
        s = (s * w_t[:, :, None, :] - sa[..., None] * (kk_t * a_t)[:, :, None, :]
             + v_t[..., None] * k_t[:, :, None, :])
        y_t = jnp.einsum('bhvk,bhk->bhv', s, r_t)
        return s, y_t
    xs = (jnp.moveaxis(r, 1, 0), jnp.moveaxis(decay, 1, 0), jnp.moveaxis(k, 1, 0),
          jnp.moveaxis(v, 1, 0), jnp.moveaxis(kk, 1, 0), jnp.moveaxis(a, 1, 0))
    s, ys = lax.scan(step, s0, xs)
    return jnp.moveaxis(ys, 0, 1), s


def _rwkv7(p, shift_prev, wkv_prev, mu_shift, w0, w_lora_up, a0, a_lora_up, g_lora_up,
           k_k, k_a, r_k, ln_x_w, ln_x_b):
    B, T, _ = p.shape
    f32 = jnp.float32
    prev = jnp.concatenate([shift_prev[:, None, :].astype(p.dtype), p[:, :-1]], axis=1)
    xs = p + (prev - p) * mu_shift
    r, k, v, wd, ad, gd = jnp.split(xs, RWKV_SPLITS, axis=-1)
    log_w = -jax.nn.softplus(-(w0 + jnp.tanh(wd) @ w_lora_up).astype(f32)) - 0.5
    decay = jnp.exp(-jnp.exp(log_w))
    a = jax.nn.sigmoid((a0 + ad @ a_lora_up).astype(f32))
    g = (jax.nn.sigmoid(gd) @ g_lora_up).astype(f32)
    hs = (B, T, RWKV_HEADS, HEAD_DIM)
    r4 = r.astype(f32).reshape(hs)
    v4 = v.astype(f32).reshape(hs)
    w4 = decay.reshape(hs)
    a4 = a.reshape(hs)
    kk = (k * k_k).astype(f32).reshape(hs)
    kk = kk * lax.rsqrt(jnp.maximum(jnp.sum(kk * kk, axis=-1, keepdims=True), 1e-24))
    k4 = k.astype(f32).reshape(hs) * (1.0 + (a4 - 1.0) * k_a.astype(f32).reshape(RWKV_HEADS, HEAD_DIM))
    y, s = _wkv_scan(r4, w4, k4, v4, kk, a4, wkv_prev.astype(f32))
    mean = jnp.mean(y, axis=-1, keepdims=True)
    var = jnp.mean(jnp.square(y - mean), axis=-1, keepdims=True)
    yn = ((y - mean) * lax.rsqrt(var + GN_EPS)).reshape(B, T, RWKV_WIDTH)
    yn = yn * ln_x_w.astype(f32) + ln_x_b.astype(f32)
    bonus = (jnp.sum(r4 * k4 * r_k.astype(f32), axis=-1, keepdims=True) * v4).reshape(B, T, RWKV_WIDTH)
    out = (yn + bonus) * g
    return out.astype(p.dtype), s


def _rel_bias(table, rel):
    idx = jnp.clip(rel, -REL_MAX, REL_MAX) + REL_MAX
    return table[:, idx].astype(jnp.float32)


def _attend(q, k, v, bias, valid):
    s = jnp.einsum('bhqd,bhkd->bhqk', q.astype(jnp.float32), k.astype(jnp.float32)) * (HEAD_DIM ** -0.5)
    s = s + bias[None]
    if valid is not None:
        s = jnp.where(valid, s, NEG_INF)
    p = jax.nn.softmax(s, axis=-1)
    return jnp.einsum('bhqk,bhkd->bhqd', p, v.astype(jnp.float32)).astype(q.dtype)


def _band_attention_prompt(q, k, v, rel_bias):
    B, H, T, Dh = q.shape
    n_chunks = T // CHUNK
    pad = ((0, 0), (0, 0), (ATT_WINDOW, 0), (0, 0))
    kp = jnp.pad(k, pad)
    vp = jnp.pad(v, pad)
    qi = jnp.arange(CHUNK)
    kj = jnp.arange(BAND)
    bias = _rel_bias(rel_bias, kj[None, :] - ATT_WINDOW - qi[:, None])

    def one_chunk(c):
        start = c * CHUNK
        qc = lax.dynamic_slice_in_dim(q, start, CHUNK, axis=2)
        kc = lax.dynamic_slice_in_dim(kp, start, BAND, axis=2)
        vc = lax.dynamic_slice_in_dim(vp, start, BAND, axis=2)
        valid = (start - ATT_WINDOW + kj)[None, :] >= 0
        return _attend(qc, kc, vc, bias, valid)

    o = lax.map(one_chunk, jnp.arange(n_chunks))
    return o.transpose(1, 0, 3, 2, 4).reshape(B, T, H * Dh)


def _band_attention_step(q, k, v, k_cache, v_cache, rel_bias):
    B, H, T, Dh = q.shape
    L = k_cache.shape[2]
    kf = jnp.concatenate([k_cache.astype(k.dtype), k], axis=2)
    vf = jnp.concatenate([v_cache.astype(v.dtype), v], axis=2)
    rel = (jnp.arange(L + T) - L)[None, :] - jnp.arange(T)[:, None]
    o = _attend(q, kf, vf, _rel_bias(rel_bias, rel), None)
    return o.transpose(0, 2, 1, 3).reshape(B, T, H * Dh)


def _layer(x, shift_prev, wkv_prev, k_cache, v_cache, w):
    (norm_ff1, w_ff1_gate, w_ff1_up, w_ff1_down, norm_mix, w_in, mu_shift, w0, w_lora_up,
     a0, a_lora_up, g_lora_up, k_k, k_a, r_k, ln_x_w, ln_x_b, rel_bias, w_out,
     norm_ff2, w_ff2_gate, w_ff2_up, w_ff2_down) = w
    B, T, _ = x.shape
    x = x + 0.5 * _swiglu(_rms_norm(x, norm_ff1), w_ff1_gate, w_ff1_up, w_ff1_down)
    h = _rms_norm(x, norm_mix)
    proj = h @ w_in
    p_rwkv = proj[..., :RWKV_COLS]
    p_att = proj[..., RWKV_COLS:]
    if shift_prev is None:
        shift_prev = jnp.zeros((B, RWKV_COLS), x.dtype)
        wkv_prev = jnp.zeros((B, RWKV_HEADS, HEAD_DIM, HEAD_DIM), jnp.float32)
    rwkv_out, wkv_new = _rwkv7(p_rwkv, shift_prev, wkv_prev, mu_shift, w0, w_lora_up, a0,
                               a_lora_up, g_lora_up, k_k, k_a, r_k, ln_x_w, ln_x_b)
    q, k, v = jnp.split(p_att, 3, axis=-1)
    q = q.reshape(B, T, ATT_HEADS, HEAD_DIM).transpose(0, 2, 1, 3)
    k = k.reshape(B, T, ATT_HEADS, HEAD_DIM).transpose(0, 2, 1, 3)
    v = v.reshape(B, T, ATT_HEADS, HEAD_DIM).transpose(0, 2, 1, 3)
    if k_cache is None:
        att = _band_attention_prompt(q, k, v, rel_bias)
        n_keep = min(ATT_WINDOW, T)
        k_rows, v_rows = k[:, :, T - n_keep:], v[:, :, T - n_keep:]
    else:
        att = _band_attention_step(q, k, v, k_cache, v_cache, rel_bias)
        k_rows, v_rows = k, v
    x = x + jnp.concatenate([rwkv_out, att], axis=-1) @ w_out
    x = x + 0.5 * _swiglu(_rms_norm(x, norm_ff2), w_ff2_gate, w_ff2_up, w_ff2_down)
    return x, p_rwkv[:, -1], wkv_new.astype(x.dtype), k_rows, v_rows


def setup_inputs(seed: int = 0) -> dict:
    key = jax.random.key(seed)
    ks = jax.random.split(key, 32)
    f32 = jnp.float32

    def nrm(k, shape, s):
        return jax.random.normal(k, shape, f32) * s

    L_cache = min(ATT_WINDOW, PAST_LEN)
    Dp = (DEPTH,)
    return {
        "x_prompt": nrm(ks[0], (BATCH, SEQ, D_MODEL), 1.0),
        "x_sample": nrm(ks[1], (DEC_BATCH, DEC_SEQ, D_MODEL), 1.0),
        "state_shift": nrm(ks[2], Dp + (DEC_BATCH, RWKV_COLS), 1.0),
        "state_wkv": nrm(ks[3], Dp + (DEC_BATCH, RWKV_HEADS, HEAD_DIM, HEAD_DIM), 0.3),
        "cache_attn_k": nrm(ks[4], Dp + (DEC_BATCH, ATT_HEADS, L_cache, HEAD_DIM), 1.0),
        "cache_attn_v": nrm(ks[5], Dp + (DEC_BATCH, ATT_HEADS, L_cache, HEAD_DIM), 1.0),
        "norm_ff1": 1.0 + nrm(ks[6], Dp + (D_MODEL,), 0.01),
        "w_ff1_gate": nrm(ks[7], Dp + (D_MODEL, D_FF), D_MODEL ** -0.5),
        "w_ff1_up": nrm(ks[8], Dp + (D_MODEL, D_FF), D_MODEL ** -0.5),
        "w_ff1_down": nrm(ks[9], Dp + (D_FF, D_MODEL), D_FF ** -0.5),
        "norm_mix": 1.0 + nrm(ks[10], Dp + (D_MODEL,), 0.01),
        "w_in": nrm(ks[11], Dp + (D_MODEL, IN_COLS), D_MODEL ** -0.5),
        "mu_shift": jax.random.uniform(ks[12], Dp + (RWKV_COLS,), f32),
        "w0": nrm(ks[13], Dp + (RWKV_WIDTH,), 0.5) - 0.5,
        "w_lora_up": nrm(ks[14], Dp + (DECAY_LORA, RWKV_WIDTH), 0.5 * DECAY_LORA ** -0.5),
        "a0": nrm(ks[15], Dp + (RWKV_WIDTH,), 0.1),
        "a_lora_up": nrm(ks[16], Dp + (ICLR_LORA, RWKV_WIDTH), 0.5 * ICLR_LORA ** -0.5),
        "g_lora_up": nrm(ks[17], Dp + (GATE_LORA, RWKV_WIDTH), GATE_LORA ** -0.5),
        "k_k": 0.85 + nrm(ks[18], Dp + (RWKV_WIDTH,), 0.02),
        "k_a": 1.0 + nrm(ks[19], Dp + (RWKV_WIDTH,), 0.02),
        "r_k": nrm(ks[20], Dp + (RWKV_HEADS, HEAD_DIM), 0.1),
        "ln_x_w": 1.0 + nrm(ks[21], Dp + (RWKV_WIDTH,), 0.01),
        "ln_x_b": nrm(ks[22], Dp + (RWKV_WIDTH,), 0.01),
        "rel_bias": nrm(ks[23], Dp + (ATT_HEADS, N_REL), 0.1),
        "w_out": nrm(ks[24], Dp + (MIX_WIDTH, D_MODEL), MIX_WIDTH ** -0.5),
        "norm_ff2": 1.0 + nrm(ks[25], Dp + (D_MODEL,), 0.01),
        "w_ff2_gate": nrm(ks[26], Dp + (D_MODEL, D_FF), D_MODEL ** -0.5),
        "w_ff2_up": nrm(ks[27], Dp + (D_MODEL, D_FF), D_MODEL ** -0.5),
        "w_ff2_down": nrm(ks[28], Dp + (D_FF, D_MODEL), D_FF ** -0.5),
        "norm_final": 1.0 + nrm(ks[29], (D_MODEL,), 0.01),
    }


def reference(x_prompt, x_sample, state_shift, state_wkv, cache_attn_k, cache_attn_v,
              norm_ff1, w_ff1_gate, w_ff1_up, w_ff1_down, norm_mix, w_in, mu_shift, w0,
              w_lora_up, a0, a_lora_up, g_lora_up, k_k, k_a, r_k, ln_x_w, ln_x_b,
              rel_bias, w_out, norm_ff2, w_ff2_gate, w_ff2_up, w_ff2_down, norm_final):
    xp = x_prompt
    xs = x_sample
    p_shift, p_wkv, p_k, p_v = [], [], [], []
    s_shift, s_wkv, s_k, s_v = [], [], [], []
    for l in range(DEPTH):
        w = (norm_ff1[l], w_ff1_gate[l], w_ff1_up[l], w_ff1_down[l], norm_mix[l], w_in[l],
             mu_shift[l], w0[l], w_lora_up[l], a0[l], a_lora_up[l], g_lora_up[l], k_k[l],
             k_a[l], r_k[l], ln_x_w[l], ln_x_b[l], rel_bias[l], w_out[l], norm_ff2[l],
             w_ff2_gate[l], w_ff2_up[l], w_ff2_down[l])
        xp, sh, wk, kr, vr = _layer(xp, None, None, None, None, w)
        p_shift.append(sh); p_wkv.append(wk); p_k.append(kr); p_v.append(vr)
        xs, sh, wk, kr, vr = _layer(xs, state_shift[l], state_wkv[l], cache_attn_k[l],
                                    cache_attn_v[l], w)
        s_shift.append(sh); s_wkv.append(wk); s_k.append(kr); s_v.append(vr)
    y_prompt = _rms_norm(xp, norm_final)
    y_sample = _rms_norm(xs, norm_final)
    return (y_prompt, y_sample,
            jnp.stack(p_shift), jnp.stack(p_wkv), jnp.stack(p_k), jnp.stack(p_v),
            jnp.stack(s_shift), jnp.stack(s_wkv), jnp.stack(s_k), jnp.stack(s_v))
```

```python
import functools
import math

import jax
import jax.numpy as jnp
from jax import lax
from jax.experimental import pallas as pl
from jax.experimental.pallas import tpu as pltpu

F32 = jnp.float32
BF16 = jnp.bfloat16

D_MODEL = 1024
HEAD_DIM = 64
RWKV_WIDTH = 512
ATT_WIDTH = 512
N_HEADS = 8
LORA_COLS = 256
RWKV_COLS = 3 * RWKV_WIDTH + LORA_COLS
ATT_COLS = 3 * ATT_WIDTH
D_FF = 2816
CHUNK = 64
ATT_WINDOW = 512
REL_MAX = 128
NORM_EPS = 1e-5
GN_EPS = 64e-5
NEG_INF = -1e30

GROUP = 256
HEADS_PER_GROUP = GROUP // HEAD_DIM
N_GROUPS = RWKV_WIDTH // GROUP
KEY_PAD = ATT_WINDOW + 2 * CHUNK

VMEM_LIMIT_BYTES = 56 * 1024 * 1024


def _rms(x, g):
    ms = jnp.mean(x * x, axis=-1, keepdims=True)
    return x * lax.rsqrt(ms + NORM_EPS) * g


def _dot(a, b):
    return jnp.dot(a, b, preferred_element_type=F32)


def _dot_nt(a, b):
    return lax.dot_general(a, b, (((1,), (1,)), ((), ())), preferred_element_type=F32)


def _swiglu(h, wg_ref, wu_ref, wd_ref):
    gate = _dot(h, wg_ref[...])
    up = _dot(h, wu_ref[...])
    act = (gate * jax.nn.sigmoid(gate) * up).astype(BF16)
    return _dot(act, wd_ref[...])


def _const_spec(shape):
    return pl.BlockSpec(shape, lambda *_: (0,) * len(shape), pipeline_mode=pl.Buffered(1))


def _ffn_inproj_kernel(x_ref, g1_ref, wg_ref, wu_ref, wd_ref, gm_ref, win_ref,
                       x1_ref, pr_ref, pa_ref):
    x = x_ref[...]
    h = _rms(x, g1_ref[...]).astype(BF16)
    x1 = x + 0.5 * _swiglu(h, wg_ref, wu_ref, wd_ref)
    x1_ref[...] = x1
    h2 = _rms(x1, gm_ref[...]).astype(BF16)
    pr_ref[...] = _dot(h2, win_ref[:, :RWKV_COLS]).astype(BF16)
    pa_ref[...] = _dot(h2, win_ref[:, RWKV_COLS:]).astype(BF16)


def _ffn_inproj(x, g1, wg, wu, wd, gm, win, tm):
    n = x.shape[0]
    row = lambda i: (i, 0)
    return pl.pallas_call(
        _ffn_inproj_kernel,
        grid=(n // tm,),
        in_specs=[pl.BlockSpec((tm, D_MODEL), row),
                  _const_spec((1, D_MODEL)), _const_spec((D_MODEL, D_FF)), _const_spec((D_MODEL, D_FF)),
                  _const_spec((D_FF, D_MODEL)), _const_spec((1, D_MODEL)),
                  _const_spec((D_MODEL, RWKV_COLS + ATT_COLS))],
        out_specs=[pl.BlockSpec((tm, D_MODEL), row), pl.BlockSpec((tm, RWKV_COLS), row),
                   pl.BlockSpec((tm, ATT_COLS), row)],
        out_shape=[jax.ShapeDtypeStruct((n, D_MODEL), F32), jax.ShapeDtypeStruct((n, RWKV_COLS), BF16),
                   jax.ShapeDtypeStruct((n, ATT_COLS), BF16)],
        compiler_params=pltpu.CompilerParams(dimension_semantics=("arbitrary",),
                                             vmem_limit_bytes=VMEM_LIMIT_BYTES),
        name="ffn_inproj",
    )(x, g1, wg, wu, wd, gm, win)


def _out_ffn_kernel(x1_ref, ro_ref, at_ref, wor_ref, woa_ref, g2_ref, wg_ref, wu_ref, wd_ref, gf_ref,
                    y_ref):
    x2 = x1_ref[...] + _dot(ro_ref[...], wor_ref[...]) + _dot(at_ref[...], woa_ref[...])
    h = _rms(x2, g2_ref[...]).astype(BF16)
    x3 = x2 + 0.5 * _swiglu(h, wg_ref, wu_ref, wd_ref)
    y_ref[...] = _rms(x3, gf_ref[...])


def _out_ffn(x1, ro, at, wor, woa, g2, wg, wu, wd, gf, tm):
    n = x1.shape[0]
    row = lambda i: (i, 0)
    return pl.pallas_call(
        _out_ffn_kernel,
        grid=(n // tm,),
        in_specs=[pl.BlockSpec((tm, D_MODEL), row), pl.BlockSpec((tm, RWKV_WIDTH), row),
                  pl.BlockSpec((tm, ATT_WIDTH), row),
                  _const_spec((RWKV_WIDTH, D_MODEL)), _const_spec((ATT_WIDTH, D_MODEL)),
                  _const_spec((1, D_MODEL)), _const_spec((D_MODEL, D_FF)), _const_spec((D_MODEL, D_FF)),
                  _const_spec((D_FF, D_MODEL)), _const_spec((1, D_MODEL))],
        out_specs=pl.BlockSpec((tm, D_MODEL), row),
        out_shape=jax.ShapeDtypeStruct((n, D_MODEL), F32),
        compiler_params=pltpu.CompilerParams(dimension_semantics=("arbitrary",),
                                             vmem_limit_bytes=VMEM_LIMIT_BYTES),
        name="out_ffn",
    )(x1, ro, at, wor, woa, g2, wg, wu, wd, gf)


def _bias_kernel(tab_ref, o_ref, *, cq):
    qi = lax.broadcasted_iota(jnp.int32, (cq, KEY_PAD), 0)
    kj = lax.broadcasted_iota(jnp.int32, (cq, KEY_PAD), 1)
    idx = jnp.clip(kj - ATT_WINDOW - qi, -REL_MAX, REL_MAX) + REL_MAX
    pad = kj >= ATT_WINDOW + cq
    for h in range(N_HEADS):
        def body(t, acc):
            return jnp.where(idx == t, tab_ref[h, t], acc)
        acc = lax.fori_loop(0, 2 * REL_MAX + 1, body, jnp.zeros((cq, KEY_PAD), F32))
        o_ref[h * cq:(h + 1) * cq, :] = jnp.where(pad, NEG_INF, acc)


def _rel_bias(table, cq):
    return pl.pallas_call(
        functools.partial(_bias_kernel, cq=cq),
        in_specs=[pl.BlockSpec(memory_space=pltpu.SMEM)],
        out_specs=pl.BlockSpec(memory_space=pltpu.VMEM),
        out_shape=jax.ShapeDtypeStruct((N_HEADS * cq, KEY_PAD), F32),
        name=f"rel_bias_{cq}",
    )(table)


def _head_masks(rows, dtype):
    lane_head = lax.broadcasted_iota(jnp.int32, (rows, GROUP), 1) >> 6
    return [(lane_head == h).astype(F32).astype(dtype) for h in range(HEADS_PER_GROUP)]


def _block_diag_rows(x, masks):
    return jnp.concatenate([x * m for m in masks], axis=0)


def _band_attn_kernel(q_ref, kp_ref, kc_ref, vp_ref, vc_ref, bias_ref, o_ref, kbuf, vbuf, *, tq):
    first_valid_row = jnp.where(pl.program_id(1) == 0, tq, 0)
    kbuf[0:tq, :] = kp_ref[0]
    kbuf[tq:2 * tq, :] = kc_ref[0]
    kbuf[2 * tq:, :] = jnp.zeros((CHUNK, ATT_WIDTH), BF16)
    vbuf[0:tq, :] = vp_ref[0]
    vbuf[tq:2 * tq, :] = vc_ref[0]
    vbuf[2 * tq:, :] = jnp.zeros((CHUNK, ATT_WIDTH), BF16)
    masks_b = _head_masks(CHUNK, BF16)
    masks_f = _head_masks(CHUNK, F32)
    kj = lax.broadcasted_iota(jnp.int32, (HEADS_PER_GROUP * CHUNK, KEY_PAD), 1)

    def chunk(cc, carry):
        r0 = pl.multiple_of(cc * CHUNK, CHUNK)
        valid = kj + r0 >= first_valid_row
        qc = q_ref[0, pl.ds(r0, CHUNK), :]
        outs = []
        for g in range(N_GROUPS):
            lanes = slice(g * GROUP, (g + 1) * GROUP)
            kg = kbuf[pl.ds(r0 + tq - ATT_WINDOW, KEY_PAD), lanes]
            vg = vbuf[pl.ds(r0 + tq - ATT_WINDOW, KEY_PAD), lanes]
            lhs = _block_diag_rows(qc[:, lanes], masks_b)
            s = _dot_nt(lhs, kg) * (HEAD_DIM ** -0.5)
            s = s + bias_ref[g * HEADS_PER_GROUP * CHUNK:(g + 1) * HEADS_PER_GROUP * CHUNK, :]
            s = jnp.where(valid, s, NEG_INF)
            p = jnp.exp(s - jnp.max(s, axis=-1, keepdims=True))
            inv = 1.0 / jnp.sum(p, axis=-1, keepdims=True)
            o_full = _dot(p.astype(BF16), vg) * inv
            o = o_full[0:CHUNK] * masks_f[0]
            for h in range(1, HEADS_PER_GROUP):
                o = o + o_full[h * CHUNK:(h + 1) * CHUNK] * masks_f[h]
            outs.append(o)
        o_ref[0, pl.ds(r0, CHUNK), :] = jnp.concatenate(outs, axis=1).astype(BF16)
        return carry

    lax.fori_loop(0, tq // CHUNK, chunk, 0)


def _band_attn(p_att, bias, tq):
    b, t, _ = p_att.shape
    assert tq == ATT_WINDOW and t % tq == 0
    cur = lambda col: (lambda bi, i: (bi, i, col))
    prev = lambda col: (lambda bi, i: (bi, jnp.maximum(i - 1, 0), col))
    blk = (1, tq, ATT_WIDTH)
    return pl.pallas_call(
        functools.partial(_band_attn_kernel, tq=tq),
        grid=(b, t // tq),
        in_specs=[pl.BlockSpec(blk, cur(0)), pl.BlockSpec(blk, prev(1)), pl.BlockSpec(blk, cur(1)),
                  pl.BlockSpec(blk, prev(2)), pl.BlockSpec(blk, cur(2)),
                  _const_spec((N_HEADS * CHUNK, KEY_PAD))],
        out_specs=pl.BlockSpec(blk, cur(0)),
        out_shape=jax.ShapeDtypeStruct((b, t, ATT_WIDTH), BF16),
        scratch_shapes=[pltpu.VMEM((2 * tq + CHUNK, ATT_WIDTH), BF16),
                        pltpu.VMEM((2 * tq + CHUNK, ATT_WIDTH), BF16)],
        compiler_params=pltpu.CompilerParams(dimension_semantics=("arbitrary", "arbitrary"),
                                             vmem_limit_bytes=VMEM_LIMIT_BYTES),
        name="band_attn",
    )(p_att, p_att, p_att, p_att, p_att, bias)


def _step_attn_kernel(q_ref, kc_ref, vc_ref, kn_ref, vn_ref, bias_ref, o_ref, *, tn):
    for h in range(N_HEADS):
        q = (q_ref[0, h] * (HEAD_DIM ** -0.5)).astype(BF16)
        b = bias_ref[h * tn:(h + 1) * tn, :]
        s1 = _dot_nt(q, kc_ref[0, h].astype(BF16)) + b[:, :ATT_WINDOW]
        s2 = _dot_nt(q, kn_ref[0, h].astype(BF16)) + b[:, ATT_WINDOW:ATT_WINDOW + tn]
        m = jnp.maximum(jnp.max(s1, axis=-1, keepdims=True), jnp.max(s2, axis=-1, keepdims=True))
        p1 = jnp.exp(s1 - m)
        p2 = jnp.exp(s2 - m)
        inv = 1.0 / (jnp.sum(p1, axis=-1, keepdims=True) + jnp.sum(p2, axis=-1, keepdims=True))
        o = _dot(p1.astype(BF16), vc_ref[0, h].astype(BF16)) + _dot(p2.astype(BF16), vn_ref[0, h].astype(BF16))
        o_ref[0, h] = o * inv


def _step_attn(q, k_cache, v_cache, k_new, v_new, bias):
    b, _, tn, _ = q.shape
    new = pl.BlockSpec((1, N_HEADS, tn, HEAD_DIM), lambda i: (i, 0, 0, 0))
    old = pl.BlockSpec((1, N_HEADS, ATT_WINDOW, HEAD_DIM), lambda i: (i, 0, 0, 0))
    return pl.pallas_call(
        functools.partial(_step_attn_kernel, tn=tn),
        grid=(b,),
        in_specs=[new, old, old, new, new, _const_spec((N_HEADS * tn, KEY_PAD))],
        out_specs=new,
        out_shape=jax.ShapeDtypeStruct((b, N_HEADS, tn, HEAD_DIM), F32),
        compiler_params=pltpu.CompilerParams(dimension_semantics=("arbitrary",)),
        name="step_attn",
    )(q, k_cache, v_cache, k_new, v_new, bias)


def _split3(x):
    hi = x.astype(BF16)
    r1 = x - hi.astype(F32)
    mid = r1.astype(BF16)
    lo = (r1 - mid.astype(F32)).astype(BF16)
    return hi, mid, lo


def _wkv_chunk(r, lam, k, v, kk, a, s_t, tri, masks_b, bd_mask, strict, incl):
    c = CHUNK
    hi, mid, lo = _split3(lam)
    cum = _dot(tri, hi) + _dot(tri, mid) + _dot(tri, lo)
    tot = cum[c - 1:c, :]
    e_cum = jnp.exp(cum)
    e_neg = jnp.exp(-cum)
    e_rem = jnp.exp(tot - cum)
    beta = a * kk
    lhs = jnp.concatenate([-kk * jnp.exp(cum - lam), r * e_cum], axis=0).astype(BF16)
    x = _dot_nt(lhs, s_t.astype(BF16))
    ab = _dot_nt(lhs, _block_diag_rows((beta * e_neg).astype(BF16), masks_b))
    ak = _dot_nt(lhs, _block_diag_rows((k * e_neg).astype(BF16), masks_b))
    zero = jnp.zeros((), F32)
    n_k = jnp.where(strict, ab[:c], zero)
    a_rb = jnp.where(incl, ab[c:], zero)
    a_ak = jnp.where(strict, ak[:c], zero)
    a_rk = jnp.where(incl, ak[c:], zero)
    v_bd = _block_diag_rows(v.astype(BF16), masks_b)
    u = x[:c] + _dot(a_ak.astype(BF16), v_bd)
    n = 1
    while True:
        n_b = n_k.astype(BF16)
        u = u + _dot(n_b, _block_diag_rows(u.astype(BF16), masks_b))
        n *= 2
        if n >= c:
            break
        n_k = _dot(n_b, _block_diag_rows(n_b, masks_b))
    u_b = u.astype(BF16)
    y = x[c:] + _dot(a_rb.astype(BF16), _block_diag_rows(u_b, masks_b)) + _dot(a_rk.astype(BF16), v_bd)
    uv_t = jnp.concatenate([u, v], axis=0).T.astype(BF16)
    bk = jnp.concatenate([beta * e_rem, k * e_rem], axis=0).astype(BF16)
    s_new = s_t * jnp.exp(tot) + _dot(uv_t, bk) * bd_mask
    return y, s_new


def _wkv_kernel(p_ref, shift_ref, s0_ref, mu_ref, w0_ref, wl_ref, a0_ref, al_ref, gl_ref, kk_ref, ka_ref,
                rk_ref, lnw_ref, lnb_ref, o_ref, s_out_ref, s_scr, prev_scr, *, tt):
    ti = pl.program_id(1)

    @pl.when(ti == 0)
    def _():
        s_scr[...] = s0_ref[0]
        prev_scr[...] = shift_ref[0]

    p = p_ref[0].astype(F32)
    row = lax.broadcasted_iota(jnp.int32, (tt, RWKV_COLS), 0)
    prev = jnp.where(row == 0, prev_scr[...], pltpu.roll(p, 1, axis=0))
    prev_scr[...] = p[tt - 1:tt, :]
    xs = p + (prev - p) * mu_ref[...]
    r = xs[:, 0:RWKV_WIDTH]
    k = xs[:, RWKV_WIDTH:2 * RWKV_WIDTH]
    v = xs[:, 2 * RWKV_WIDTH:3 * RWKV_WIDTH]
    lora_in = xs[:, 3 * RWKV_WIDTH:3 * RWKV_WIDTH + 128]
    gate_in = xs[:, 3 * RWKV_WIDTH + 128:]
    z = w0_ref[...] + _dot(jnp.tanh(lora_in).astype(BF16), wl_ref[...])
    lam = -math.exp(-0.5) * jax.nn.sigmoid(z)
    a = jax.nn.sigmoid(a0_ref[...] + _dot(lora_in.astype(BF16), al_ref[...]))
    g = _dot(jax.nn.sigmoid(gate_in).astype(BF16), gl_ref[...])

    ri = lax.broadcasted_iota(jnp.int32, (RWKV_WIDTH, RWKV_WIDTH), 0) >> 6
    ci = lax.broadcasted_iota(jnp.int32, (RWKV_WIDTH, RWKV_WIDTH), 1) >> 6
    head_ones = (ri == ci).astype(F32).astype(BF16)
    kk = k * kk_ref[...]
    kk = kk * lax.rsqrt(jnp.maximum(_dot((kk * kk).astype(BF16), head_ones), 1e-24))
    k = k * (1.0 + (a - 1.0) * ka_ref[...])

    c = CHUNK
    t_r = lax.broadcasted_iota(jnp.int32, (c, c), 0)
    t_c = lax.broadcasted_iota(jnp.int32, (c, c), 1)
    tri = (t_c <= t_r).astype(F32).astype(BF16)
    t_row = lax.broadcasted_iota(jnp.int32, (c, GROUP), 0)
    s_col = lax.broadcasted_iota(jnp.int32, (c, GROUP), 1) & (c - 1)
    strict = s_col < t_row
    incl = s_col <= t_row
    masks_b = _head_masks(c, BF16)
    bd_mask = ((lax.broadcasted_iota(jnp.int32, (GROUP, GROUP), 0) >> 6)
               == (lax.broadcasted_iota(jnp.int32, (GROUP, GROUP), 1) >> 6)).astype(F32)

    n_chunks = -(-tt // c)
    pad = n_chunks * c - tt

    def padded(x):
        return x if pad == 0 else jnp.concatenate([x, jnp.zeros((pad, x.shape[1]), F32)], axis=0)

    rp, lp, kp, vp, kkp, ap = (padded(t) for t in (r, lam, k, v, kk, a))
    ys = []
    for g_i in range(N_GROUPS):
        lanes = slice(g_i * GROUP, (g_i + 1) * GROUP)
        s_t = s_scr[g_i]
        ys_g = []
        for ci_ in range(n_chunks):
            rows = slice(ci_ * c, (ci_ + 1) * c)
            y_c, s_t = _wkv_chunk(rp[rows, lanes], lp[rows, lanes], kp[rows, lanes], vp[rows, lanes],
                                  kkp[rows, lanes], ap[rows, lanes], s_t, tri, masks_b, bd_mask,
                                  strict, incl)
            ys_g.append(y_c)
        s_scr[g_i] = s_t
        ys.append(ys_g[0] if n_chunks == 1 else jnp.concatenate(ys_g, axis=0))
    y = jnp.concatenate(ys, axis=1)[:tt]

    inv_n = 1.0 / HEAD_DIM
    mean = _dot(y.astype(BF16), head_ones) * inv_n
    d = y - mean
    var = _dot((d * d).astype(BF16), head_ones) * inv_n
    yn = d * lax.rsqrt(var + GN_EPS) * lnw_ref[...] + lnb_ref[...]
    bonus = _dot((r * k * rk_ref[...]).astype(BF16), head_ones) * v
    o_ref[0] = ((yn + bonus) * g).astype(BF16)

    @pl.when(ti == pl.num_programs(1) - 1)
    def _():
        s_out_ref[0] = s_scr[...]


def _wkv(p_rwkv, shift0, s0, params, tt):
    b, t, _ = p_rwkv.shape
    vec = lambda n: _const_spec((1, n))
    return pl.pallas_call(
        functools.partial(_wkv_kernel, tt=tt),
        grid=(b, t // tt),
        in_specs=[pl.BlockSpec((1, tt, RWKV_COLS), lambda bi, i: (bi, i, 0)),
                  pl.BlockSpec((1, 1, RWKV_COLS), lambda bi, i: (bi, 0, 0)),
                  pl.BlockSpec((1, N_GROUPS, GROUP, GROUP), lambda bi, i: (bi, 0, 0, 0)),
                  vec(RWKV_COLS), vec(RWKV_WIDTH), _const_spec((128, RWKV_WIDTH)), vec(RWKV_WIDTH),
                  _const_spec((128, RWKV_WIDTH)), _const_spec((128, RWKV_WIDTH)),
                  vec(RWKV_WIDTH), vec(RWKV_WIDTH), vec(RWKV_WIDTH), vec(RWKV_WIDTH), vec(RWKV_WIDTH)],
        out_specs=[pl.BlockSpec((1, tt, RWKV_WIDTH), lambda bi, i: (bi, i, 0)),
                   pl.BlockSpec((1, N_GROUPS, GROUP, GROUP), lambda bi, i: (bi, 0, 0, 0))],
        out_shape=[jax.ShapeDtypeStruct((b, t, RWKV_WIDTH), BF16),
                   jax.ShapeDtypeStruct((b, N_GROUPS, GROUP, GROUP), F32)],
        scratch_shapes=[pltpu.VMEM((N_GROUPS, GROUP, GROUP), F32), pltpu.VMEM((1, RWKV_COLS), F32)],
        compiler_params=pltpu.CompilerParams(dimension_semantics=("arbitrary", "arbitrary"),
                                             vmem_limit_bytes=VMEM_LIMIT_BYTES),
        name="wkv",
    )(p_rwkv, shift0, s0, *params)


def _embed_states(s):
    b = s.shape[0]
    eye = jnp.eye(HEADS_PER_GROUP, dtype=s.dtype)
    s = s.reshape(b, N_GROUPS, HEADS_PER_GROUP, HEAD_DIM, HEAD_DIM)
    bd = s[:, :, :, :, None, :] * eye[None, None, :, None, :, None]
    return bd.reshape(b, N_GROUPS, GROUP, GROUP)


def _extract_states(bd):
    b = bd.shape[0]
    x = bd.reshape(b, N_GROUPS, HEADS_PER_GROUP, HEAD_DIM, HEADS_PER_GROUP, HEAD_DIM)
    blocks = [x[:, :, h, :, h, :] for h in range(HEADS_PER_GROUP)]
    return jnp.stack(blocks, axis=2).reshape(b, N_HEADS, HEAD_DIM, HEAD_DIM)


def _heads_first(x):
    b, t, _ = x.shape
    return x.reshape(b, t, N_HEADS, HEAD_DIM).transpose(0, 2, 1, 3)


def _token_tile(n):
    for tm in (256, 128, 64, 32, 16, 8):
        if n % tm == 0:
            return tm
    raise ValueError(n)


def kernel(x_prompt, x_sample, state_shift, state_wkv, cache_attn_k, cache_attn_v, norm_ff1, w_ff1_gate, w_ff1_up, w_ff1_down, norm_mix, w_in, mu_shift, w0, w_lora_up, a0, a_lora_up, g_lora_up, k_k, k_a, r_k, ln_x_w, ln_x_b, rel_bias, w_out, norm_ff2, w_ff2_gate, w_ff2_up, w_ff2_down, norm_final):
    assert norm_ff1.shape[0] == 1, "single layer"
    bp, tp, _ = x_prompt.shape
    bs, ts, _ = x_sample.shape
    assert tp % ATT_WINDOW == 0 and ts <= CHUNK and cache_attn_k.shape[3] == ATT_WINDOW

    row = lambda w: w[0].reshape(1, -1).astype(F32)
    bf = lambda w: w[0].astype(BF16)
    ffn1 = (row(norm_ff1), bf(w_ff1_gate), bf(w_ff1_up), bf(w_ff1_down))
    ffn2 = (row(norm_ff2), bf(w_ff2_gate), bf(w_ff2_up), bf(w_ff2_down))
    w_out_b = bf(w_out)
    norm_final_row = norm_final.reshape(1, D_MODEL).astype(F32)
    lora_pad = jnp.zeros((HEAD_DIM, RWKV_WIDTH), BF16)
    wkv_params = (row(mu_shift), row(w0), jnp.concatenate([bf(w_lora_up), lora_pad], axis=0), row(a0),
                  jnp.concatenate([lora_pad, bf(a_lora_up)], axis=0), bf(g_lora_up),
                  row(k_k), row(k_a), row(r_k), row(ln_x_w), row(ln_x_b))

    def front(x):
        b, t, _ = x.shape
        tm = _token_tile(b * t)
        x1, pr, pa = _ffn_inproj(x.reshape(b * t, D_MODEL), *ffn1, row(norm_mix), bf(w_in), tm)
        return x1, pr.reshape(b, t, RWKV_COLS), pa.reshape(b, t, ATT_COLS)

    def back(x1, ro, at, b, t):
        tm = _token_tile(b * t)
        y = _out_ffn(x1, ro.reshape(b * t, RWKV_WIDTH), at.reshape(b * t, ATT_WIDTH),
                     w_out_b[:RWKV_WIDTH], w_out_b[RWKV_WIDTH:], *ffn2, norm_final_row, tm)
        return y.reshape(b, t, D_MODEL)

    x1, pr, pa = front(x_prompt)
    ro, s_bd = _wkv(pr, jnp.zeros((bp, 1, RWKV_COLS), F32), jnp.zeros((bp, N_GROUPS, GROUP, GROUP), F32),
                    wkv_params, 256)
    at = _band_attn(pa, _rel_bias(rel_bias[0], CHUNK), ATT_WINDOW)
    y_prompt = back(x1, ro, at, bp, tp)
    p_shift = pr[:, -1, :].astype(F32)[None]
    p_wkv = _extract_states(s_bd)[None]
    p_k = _heads_first(pa[:, tp - ATT_WINDOW:, ATT_WIDTH:2 * ATT_WIDTH].astype(F32))[None]
    p_v = _heads_first(pa[:, tp - ATT_WINDOW:, 2 * ATT_WIDTH:].astype(F32))[None]

    x1, pr, pa = front(x_sample)
    ro, s_bd = _wkv(pr, state_shift[0][:, None, :], _embed_states(state_wkv[0]), wkv_params, ts)
    q = _heads_first(pa[:, :, :ATT_WIDTH].astype(F32))
    s_k = _heads_first(pa[:, :, ATT_WIDTH:2 * ATT_WIDTH].astype(F32))
    s_v = _heads_first(pa[:, :, 2 * ATT_WIDTH:].astype(F32))
    at = _step_attn(q, cache_attn_k[0], cache_attn_v[0], s_k, s_v, _rel_bias(rel_bias[0], ts))
    at = at.transpose(0, 2, 1, 3).reshape(bs, ts, ATT_WIDTH).astype(BF16)
    y_sample = back(x1, ro, at, bs, ts)
    s_shift = pr[:, -1, :].astype(F32)[None]
    s_wkv = _extract_states(s_bd)[None]

    return (y_prompt, y_sample, p_shift, p_wkv, p_k, p_v, s_shift, s_wkv, s_k[None], s_v[None])
```

```python
import functools
import math

import numpy as np
import jax
import jax.numpy as jnp
from jax import lax
from jax.experimental import pallas as pl
from jax.experimental.pallas import tpu as pltpu

F32 = jnp.float32
BF16 = jnp.bfloat16

D_MODEL = 1024
HEAD_DIM = 64
RWKV_WIDTH = 512
ATT_WIDTH = 512
N_HEADS = 8
LORA_COLS = 256
RWKV_COLS = 3 * RWKV_WIDTH + LORA_COLS
ATT_COLS = 3 * ATT_WIDTH
D_FF = 2816
CHUNK = 64
ATT_WINDOW = 512
REL_MAX = 128
NORM_EPS = 1e-5
GN_EPS = 64e-5
NEG_INF = -1e30

GROUP = 256
HEADS_PER_GROUP = GROUP // HEAD_DIM
N_GROUPS = RWKV_WIDTH // GROUP
KEY_PAD = ATT_WINDOW + 2 * CHUNK
LOG2_E = math.log2(math.e)
QK_SCALE = HEAD_DIM ** -0.5 * LOG2_E
ATT_CHUNKS_PER_ITER = 2

TOKEN_TILE = 256
WKV_BATCH = 2
WKV_TILE = 128
VMEM_LIMIT_BYTES = 56 * 1024 * 1024


def _rms(x, g):
    ms = jnp.mean(x * x, axis=-1, keepdims=True)
    return x * lax.rsqrt(ms + NORM_EPS) * g


def _dot(a, b):
    return jnp.dot(a, b, preferred_element_type=F32)


def _dot_nt(a, b):
    return lax.dot_general(a, b, (((1,), (1,)), ((), ())), preferred_element_type=F32)


def _swiglu(h, wg_ref, wu_ref, wd_ref):
    gate = _dot(h, wg_ref[...])
    up = _dot(h, wu_ref[...])
    act = (gate * jax.nn.sigmoid(gate) * up).astype(BF16)
    return _dot(act, wd_ref[...])


def _const_spec(shape):
    return pl.BlockSpec(shape, lambda *_: (0,) * len(shape), pipeline_mode=pl.Buffered(1))


def _head_block_ones(n):
    h = np.arange(n) // HEAD_DIM
    return (h[:, None] == h[None, :]).astype(np.float32)


def _ffn_inproj_kernel(x_ref, first_ref, g1_ref, wg_ref, wu_ref, wd_ref, gm_ref, win_ref,
                       mu_ref, w0_ref, wl_ref, a0_ref, al_ref, gl_ref, kkw_ref, ka_ref, rk_ref, ones_ref,
                       x1_ref, pa_ref, r_ref, k_ref, v_ref, kk_ref, beta_ref, lam_ref, g_ref, bonus_ref,
                       plast_ref, carry_scr, *, tm, t_seq):
    x = x_ref[...]
    h = _rms(x, g1_ref[...]).astype(BF16)
    x1 = x + 0.5 * _swiglu(h, wg_ref, wu_ref, wd_ref)
    x1_ref[...] = x1
    h2 = _rms(x1, gm_ref[...]).astype(BF16)
    q_cols = slice(RWKV_COLS, RWKV_COLS + ATT_WIDTH)
    pa_ref[:, :ATT_WIDTH] = (_dot(h2, win_ref[:, q_cols]) * QK_SCALE).astype(BF16)
    pa_ref[:, ATT_WIDTH:] = _dot(h2, win_ref[:, RWKV_COLS + ATT_WIDTH:]).astype(BF16)
    p = _dot(h2, win_ref[:, :RWKV_COLS])

    row = lax.broadcasted_iota(jnp.int32, (tm, RWKV_COLS), 0)
    rolled = pltpu.roll(p, 1, axis=0)
    if t_seq >= tm:
        @pl.when(pl.program_id(0) % (t_seq // tm) == 0)
        def _():
            carry_scr[...] = first_ref[0]
        prev = jnp.where(row == 0, carry_scr[...], rolled)
        carry_scr[...] = p[tm - 1:tm, :]
        plast_ref[0] = p[tm - 1:tm, :]
    else:
        prev = jnp.where((row & (t_seq - 1)) == 0, first_ref[...], rolled)
        plast_ref[...] = p
    xs = p + (prev - p) * mu_ref[...]

    r = xs[:, 0:RWKV_WIDTH]
    k = xs[:, RWKV_WIDTH:2 * RWKV_WIDTH]
    v = xs[:, 2 * RWKV_WIDTH:3 * RWKV_WIDTH]
    lora_in = xs[:, 3 * RWKV_WIDTH:3 * RWKV_WIDTH + 128]
    gate_in = xs[:, 3 * RWKV_WIDTH + 128:]
    z = w0_ref[...] + _dot(jnp.tanh(lora_in).astype(BF16), wl_ref[...])
    lam_ref[...] = -math.exp(-0.5) * jax.nn.sigmoid(z)
    a = jax.nn.sigmoid(a0_ref[...] + _dot(lora_in.astype(BF16), al_ref[...]))
    g_ref[...] = _dot(jax.nn.sigmoid(gate_in).astype(BF16), gl_ref[...]).astype(BF16)
    kk = k * kkw_ref[...]
    kk = kk * lax.rsqrt(jnp.maximum(_dot((kk * kk).astype(BF16), ones_ref[...]), 1e-24))
    k = k * (1.0 + (a - 1.0) * ka_ref[...])
    r_ref[...] = r.astype(BF16)
    k_ref[...] = k.astype(BF16)
    v_ref[...] = v.astype(BF16)
    kk_ref[...] = kk.astype(BF16)
    beta_ref[...] = (a * kk).astype(BF16)
    bonus_ref[...] = (_dot((r * k * rk_ref[...]).astype(BF16), ones_ref[...]) * v).astype(BF16)


def _ffn_inproj(x, first, ffn, gm, win, gates, ones, t_seq):
    n = x.shape[0]
    tm = min(TOKEN_TILE, n)
    assert n % tm == 0 and (t_seq % tm == 0 or tm % t_seq == 0) and t_seq & (t_seq - 1) == 0
    row = lambda i: (i, 0)
    vec = lambda m: _const_spec((1, m))
    wide = lambda dt: jax.ShapeDtypeStruct((n, RWKV_WIDTH), dt)
    wide_spec = pl.BlockSpec((tm, RWKV_WIDTH), row)
    if t_seq >= tm:
        first_spec = pl.BlockSpec((1, 1, RWKV_COLS), lambda i: (i // (t_seq // tm), 0, 0))
        plast_shape = jax.ShapeDtypeStruct((n // tm, 1, RWKV_COLS), F32)
        plast_spec = pl.BlockSpec((1, 1, RWKV_COLS), lambda i: (i, 0, 0))
    else:
        first_spec = pl.BlockSpec((tm, RWKV_COLS), row)
        plast_shape = jax.ShapeDtypeStruct((n, RWKV_COLS), F32)
        plast_spec = pl.BlockSpec((tm, RWKV_COLS), row)
    return pl.pallas_call(
        functools.partial(_ffn_inproj_kernel, tm=tm, t_seq=t_seq),
        grid=(n // tm,),
        in_specs=[pl.BlockSpec((tm, D_MODEL), row), first_spec,
                  vec(D_MODEL), _const_spec((D_MODEL, D_FF)), _const_spec((D_MODEL, D_FF)),
                  _const_spec((D_FF, D_MODEL)), vec(D_MODEL), _const_spec((D_MODEL, RWKV_COLS + ATT_COLS)),
                  vec(RWKV_COLS), vec(RWKV_WIDTH), _const_spec((128, RWKV_WIDTH)), vec(RWKV_WIDTH),
                  _const_spec((128, RWKV_WIDTH)), _const_spec((128, RWKV_WIDTH)),
                  vec(RWKV_WIDTH), vec(RWKV_WIDTH), vec(RWKV_WIDTH), _const_spec((RWKV_WIDTH, RWKV_WIDTH))],
        out_specs=[pl.BlockSpec((tm, D_MODEL), row), pl.BlockSpec((tm, ATT_COLS), row)]
                  + [wide_spec] * 8 + [plast_spec],
        out_shape=[jax.ShapeDtypeStruct((n, D_MODEL), F32), jax.ShapeDtypeStruct((n, ATT_COLS), BF16),
                   wide(BF16), wide(BF16), wide(BF16), wide(BF16), wide(BF16), wide(F32), wide(BF16),
                   wide(BF16), plast_shape],
        scratch_shapes=[pltpu.VMEM((1, RWKV_COLS), F32)],
        compiler_params=pltpu.CompilerParams(dimension_semantics=("arbitrary",),
                                             vmem_limit_bytes=VMEM_LIMIT_BYTES),
        name="ffn_inproj",
    )(x, first, *ffn, gm, win, *gates, ones)


def _out_ffn_kernel(x1_ref, y_ref, bonus_ref, g_ref, at_ref, lnw_ref, lnb_ref, ones_ref, wor_ref, woa_ref,
                    g2_ref, wg_ref, wu_ref, wd_ref, gf_ref, o_ref):
    y = y_ref[...].astype(F32)
    inv_n = 1.0 / HEAD_DIM
    d = y - _dot(y_ref[...], ones_ref[...]) * inv_n
    var = _dot((d * d).astype(BF16), ones_ref[...]) * inv_n
    yn = d * lax.rsqrt(var + GN_EPS) * lnw_ref[...] + lnb_ref[...]
    ro = ((yn + bonus_ref[...].astype(F32)) * g_ref[...].astype(F32)).astype(BF16)
    x2 = x1_ref[...] + _dot(ro, wor_ref[...]) + _dot(at_ref[...], woa_ref[...])
    h = _rms(x2, g2_ref[...]).astype(BF16)
    x3 = x2 + 0.5 * _swiglu(h, wg_ref, wu_ref, wd_ref)
    o_ref[...] = _rms(x3, gf_ref[...])


def _out_ffn(x1, y, bonus, g, at, lnw, lnb, ones, wor, woa, ffn, gf):
    n = x1.shape[0]
    tm = min(TOKEN_TILE, n)
    row = lambda i: (i, 0)
    vec = lambda m: _const_spec((1, m))
    wide_spec = pl.BlockSpec((tm, RWKV_WIDTH), row)
    return pl.pallas_call(
        _out_ffn_kernel,
        grid=(n // tm,),
        in_specs=[pl.BlockSpec((tm, D_MODEL), row), wide_spec, wide_spec, wide_spec, wide_spec,
                  vec(RWKV_WIDTH), vec(RWKV_WIDTH), _const_spec((RWKV_WIDTH, RWKV_WIDTH)),
                  _const_spec((RWKV_WIDTH, D_MODEL)), _const_spec((ATT_WIDTH, D_MODEL)),
                  vec(D_MODEL), _const_spec((D_MODEL, D_FF)), _const_spec((D_MODEL, D_FF)),
                  _const_spec((D_FF, D_MODEL)), vec(D_MODEL)],
        out_specs=pl.BlockSpec((tm, D_MODEL), row),
        out_shape=jax.ShapeDtypeStruct((n, D_MODEL), F32),
        compiler_params=pltpu.CompilerParams(dimension_semantics=("arbitrary",),
                                             vmem_limit_bytes=VMEM_LIMIT_BYTES),
        name="out_ffn",
    )(x1, y, bonus, g, at, lnw, lnb, ones, wor, woa, *ffn, gf)


def _bias_kernel(tab_ref, o_ref, *, cq):
    qi = lax.broadcasted_iota(jnp.int32, (cq, KEY_PAD), 0)
    kj = lax.broadcasted_iota(jnp.int32, (cq, KEY_PAD), 1)
    idx = jnp.clip(kj - ATT_WINDOW - qi, -REL_MAX, REL_MAX) + REL_MAX
    pad = kj >= ATT_WINDOW + cq
    for h in range(N_HEADS):
        def body(t, acc):
            return jnp.where(idx == t, tab_ref[h, t], acc)
        acc = lax.fori_loop(0, 2 * REL_MAX + 1, body, jnp.zeros((cq, KEY_PAD), F32))
        o_ref[h * cq:(h + 1) * cq, :] = jnp.where(pad, NEG_INF, acc * LOG2_E)


def _rel_bias(table, cq):
    return pl.pallas_call(
        functools.partial(_bias_kernel, cq=cq),
        in_specs=[pl.BlockSpec(memory_space=pltpu.SMEM)],
        out_specs=pl.BlockSpec(memory_space=pltpu.VMEM),
        out_shape=jax.ShapeDtypeStruct((N_HEADS * cq, KEY_PAD), F32),
        name=f"rel_bias_{cq}",
    )(table)


def _head_masks(rows, dtype):
    lane_head = lax.broadcasted_iota(jnp.int32, (rows, GROUP), 1) >> 6
    return [(lane_head == h).astype(F32).astype(dtype) for h in range(HEADS_PER_GROUP)]


def _block_diag_rows(x, masks):
    return jnp.concatenate([x * m for m in masks], axis=0)


def _band_attn_kernel(q_ref, kp_ref, kc_ref, vp_ref, vc_ref, bias_ref, o_ref, kbuf, vbuf, *, tq):
    kbuf[0:tq, :] = kp_ref[0]
    kbuf[tq:2 * tq, :] = kc_ref[0]
    kbuf[2 * tq:, :] = jnp.zeros((CHUNK, ATT_WIDTH), BF16)
    vbuf[0:tq, :] = vp_ref[0]
    vbuf[tq:2 * tq, :] = vc_ref[0]
    vbuf[2 * tq:, :] = jnp.zeros((CHUNK, ATT_WIDTH), BF16)
    masks_b = _head_masks(CHUNK, BF16)
    masks_f = _head_masks(CHUNK, F32)
    rows = HEADS_PER_GROUP * CHUNK
    cells = [(j, g) for j in range(ATT_CHUNKS_PER_ITER) for g in range(N_GROUPS)]
    lanes = [slice(g * GROUP, (g + 1) * GROUP) for g in range(N_GROUPS)]

    def run(first_tile):
        def body(it, carry):
            r0 = [pl.multiple_of((it * ATT_CHUNKS_PER_ITER + j) * CHUNK, CHUNK)
                  for j in range(ATT_CHUNKS_PER_ITER)]
            win = [pl.ds(r + tq - ATT_WINDOW, KEY_PAD) for r in r0]
            lhs = {(j, g): _block_diag_rows(q_ref[0, pl.ds(r0[j], CHUNK), lanes[g]], masks_b)
                   for (j, g) in cells}
            s = {(j, g): _dot_nt(lhs[(j, g)], kbuf[win[j], lanes[g]]) for (j, g) in cells}
            p, inv = {}, {}
            for (j, g) in cells:
                sz = s[(j, g)] + bias_ref[g * rows:(g + 1) * rows, :]
                if first_tile:
                    kj = lax.broadcasted_iota(jnp.int32, (rows, KEY_PAD), 1)
                    sz = jnp.where(kj + r0[j] >= tq, sz, NEG_INF)
                pz = jnp.exp2(sz - jnp.max(sz, axis=-1, keepdims=True))
                inv[(j, g)] = 1.0 / jnp.sum(pz, axis=-1, keepdims=True)
                p[(j, g)] = pz.astype(BF16)
            o_full = {(j, g): _dot(p[(j, g)], vbuf[win[j], lanes[g]]) * inv[(j, g)]
                      for (j, g) in cells}
            for j in range(ATT_CHUNKS_PER_ITER):
                outs = []
                for g in range(N_GROUPS):
                    o = o_full[(j, g)][0:CHUNK] * masks_f[0]
                    for h in range(1, HEADS_PER_GROUP):
                        o = o + o_full[(j, g)][h * CHUNK:(h + 1) * CHUNK] * masks_f[h]
                    outs.append(o)
                o_ref[0, pl.ds(r0[j], CHUNK), :] = jnp.concatenate(outs, axis=1).astype(BF16)
            return carry

        lax.fori_loop(0, tq // CHUNK // ATT_CHUNKS_PER_ITER, body, 0)

    @pl.when(pl.program_id(1) == 0)
    def _():
        run(True)

    @pl.when(pl.program_id(1) > 0)
    def _():
        run(False)


def _band_attn(p_att, bias, tq):
    b, t, _ = p_att.shape
    assert tq == ATT_WINDOW and t % tq == 0
    cur = lambda col: (lambda bi, i: (bi, i, col))
    prev = lambda col: (lambda bi, i: (bi, jnp.maximum(i - 1, 0), col))
    blk = (1, tq, ATT_WIDTH)
    return pl.pallas_call(
        functools.partial(_band_attn_kernel, tq=tq),
        grid=(b, t // tq),
        in_specs=[pl.BlockSpec(blk, cur(0)), pl.BlockSpec(blk, prev(1)), pl.BlockSpec(blk, cur(1)),
                  pl.BlockSpec(blk, prev(2)), pl.BlockSpec(blk, cur(2)),
                  _const_spec((N_HEADS * CHUNK, KEY_PAD))],
        out_specs=pl.BlockSpec(blk, cur(0)),
        out_shape=jax.ShapeDtypeStruct((b, t, ATT_WIDTH), BF16),
        scratch_shapes=[pltpu.VMEM((2 * tq + CHUNK, ATT_WIDTH), BF16),
                        pltpu.VMEM((2 * tq + CHUNK, ATT_WIDTH), BF16)],
        compiler_params=pltpu.CompilerParams(dimension_semantics=("arbitrary", "arbitrary"),
                                             vmem_limit_bytes=VMEM_LIMIT_BYTES),
        name="band_attn",
    )(p_att, p_att, p_att, p_att, p_att, bias)


def _step_attn_kernel(q_ref, kc_ref, vc_ref, kn_ref, vn_ref, bias_ref, o_ref, *, tn):
    for h in range(N_HEADS):
        q = q_ref[0, h].astype(BF16)
        b = bias_ref[h * tn:(h + 1) * tn, :]
        s1 = _dot_nt(q, kc_ref[0, h].astype(BF16)) + b[:, :ATT_WINDOW]
        s2 = _dot_nt(q, kn_ref[0, h].astype(BF16)) + b[:, ATT_WINDOW:ATT_WINDOW + tn]
        m = jnp.maximum(jnp.max(s1, axis=-1, keepdims=True), jnp.max(s2, axis=-1, keepdims=True))
        p1 = jnp.exp2(s1 - m)
        p2 = jnp.exp2(s2 - m)
        inv = 1.0 / (jnp.sum(p1, axis=-1, keepdims=True) + jnp.sum(p2, axis=-1, keepdims=True))
        o = _dot(p1.astype(BF16), vc_ref[0, h].astype(BF16)) + _dot(p2.astype(BF16), vn_ref[0, h].astype(BF16))
        o_ref[0, h] = o * inv


def _step_attn(q, k_cache, v_cache, k_new, v_new, bias):
    b, _, tn, _ = q.shape
    new = pl.BlockSpec((1, N_HEADS, tn, HEAD_DIM), lambda i: (i, 0, 0, 0))
    old = pl.BlockSpec((1, N_HEADS, ATT_WINDOW, HEAD_DIM), lambda i: (i, 0, 0, 0))
    return pl.pallas_call(
        functools.partial(_step_attn_kernel, tn=tn),
        grid=(b,),
        in_specs=[new, old, old, new, new, _const_spec((N_HEADS * tn, KEY_PAD))],
        out_specs=new,
        out_shape=jax.ShapeDtypeStruct((b, N_HEADS, tn, HEAD_DIM), F32),
        compiler_params=pltpu.CompilerParams(dimension_semantics=("arbitrary",)),
        name="step_attn",
    )(q, k_cache, v_cache, k_new, v_new, bias)


def _split3(x):
    hi = x.astype(BF16)
    r1 = x - hi.astype(F32)
    mid = r1.astype(BF16)
    lo = (r1 - mid.astype(F32)).astype(BF16)
    return hi, mid, lo


def _wkv_kernel(r_ref, k_ref, v_ref, kk_ref, beta_ref, lam_ref, s0_ref, tri3_ref, hm_ref, bdm_ref, eye4_ref,
                y_ref, s_out_ref, s_scr, *, nb, tt):
    ti = pl.program_id(1)

    @pl.when(ti == 0)
    def _():
        s_scr[...] = s0_ref[...]

    c = CHUNK
    t_row = lax.broadcasted_iota(jnp.int32, (c, GROUP), 0)
    s_col = lax.broadcasted_iota(jnp.int32, (c, GROUP), 1) & (c - 1)
    strict = s_col < t_row
    incl = s_col <= t_row
    tri3, hm, bdm, eye4 = tri3_ref[...], hm_ref[...], bdm_ref[...], eye4_ref[...]
    zero = jnp.zeros((), F32)
    n_chunks = -(-tt // c)
    pad = n_chunks * c - tt

    def bd(x):
        return jnp.concatenate([x] * HEADS_PER_GROUP, axis=0) * hm

    def padded(x):
        return x if pad == 0 else jnp.concatenate([x, jnp.zeros((pad, x.shape[1]), x.dtype)], axis=0)

    lanes_of = [(b, g) for b in range(nb) for g in range(N_GROUPS)]
    cells = [(b, g, ci) for ci in range(n_chunks) for (b, g) in lanes_of]

    def cut(ref, cell, cast):
        b, g, ci = cell
        lanes = slice(g * GROUP, (g + 1) * GROUP)
        x = ref[b, ci * c:(ci + 1) * c, lanes] if pad == 0 else padded(ref[b, :, lanes])
        return x.astype(F32) if cast else x

    lam = {z: cut(lam_ref, z, False) for z in cells}
    cum = {z: _dot(tri3, jnp.concatenate(_split3(lam[z]), axis=0)) for z in cells}
    tot = {z: cum[z][c - 1:c, :] for z in cells}
    lhs, bt, kt, bk, v_bd = {}, {}, {}, {}, {}
    for z in cells:
        r, k, kk, beta = (cut(ref, z, True) for ref in (r_ref, k_ref, kk_ref, beta_ref))
        e_neg = jnp.exp(-cum[z])
        e_rem = jnp.exp(tot[z] - cum[z])
        lhs[z] = jnp.concatenate([-kk * jnp.exp(cum[z] - lam[z]), r * jnp.exp(cum[z])],
                                 axis=0).astype(BF16)
        bt[z] = bd((beta * e_neg).astype(BF16))
        kt[z] = bd((k * e_neg).astype(BF16))
        bk[z] = jnp.concatenate([beta * e_rem, k * e_rem], axis=0).astype(BF16)
        v_bd[z] = bd(cut(v_ref, z, False))
    ab = {z: _dot_nt(lhs[z], bt[z]) for z in cells}
    ak = {z: _dot_nt(lhs[z], kt[z]) for z in cells}
    a_rb = {z: jnp.where(incl, ab[z][c:], zero).astype(BF16) for z in cells}
    a_ak = {z: jnp.where(strict, ak[z][:c], zero).astype(BF16) for z in cells}
    a_rk = {z: jnp.where(incl, ak[z][c:], zero).astype(BF16) for z in cells}
    n_k = {z: jnp.where(strict, ab[z][:c], zero) for z in cells}
    n_b = {z: n_k[z].astype(BF16) for z in cells}
    t_m = {z: eye4 + n_k[z] for z in cells}
    n_b = {z: _dot(n_b[z], bd(n_b[z])).astype(BF16) for z in cells}
    for _ in range(int(math.log2(c)) - 2):
        both = {z: _dot(jnp.concatenate([t_m[z].astype(BF16), n_b[z]], axis=0), bd(n_b[z])) for z in cells}
        t_m = {z: t_m[z] + both[z][:c] for z in cells}
        n_b = {z: both[z][c:].astype(BF16) for z in cells}
    t_b = {z: (t_m[z] + _dot(t_m[z].astype(BF16), bd(n_b[z]))).astype(BF16) for z in cells}
    wy_v = {z: _dot(jnp.concatenate([a_ak[z], a_rk[z]], axis=0), v_bd[z]) for z in cells}
    w_v = {z: wy_v[z][:c] for z in cells}
    y_v = {z: wy_v[z][c:] for z in cells}

    s_t = {bg: s_scr[bg[0], bg[1]] for bg in lanes_of}
    y = {}
    for ci in range(n_chunks):
        zs = [(b, g, ci) for (b, g) in lanes_of]
        x = {z: _dot_nt(lhs[z], s_t[z[:2]].astype(BF16)) for z in zs}
        u = {z: _dot(t_b[z], bd((x[z][:c] + w_v[z]).astype(BF16))) for z in zs}
        for z in zs:
            uv_t = jnp.concatenate([u[z], cut(v_ref, z, True)], axis=0).T.astype(BF16)
            s_t[z[:2]] = s_t[z[:2]] * jnp.exp(tot[z]) + _dot(uv_t, bk[z]) * bdm
        for z in zs:
            y[z] = x[z][c:] + y_v[z] + _dot(a_rb[z], bd(u[z].astype(BF16)))
    for (b, g) in lanes_of:
        s_scr[b, g] = s_t[(b, g)]
    for b in range(nb):
        rows = [jnp.concatenate([y[(b, g, ci)] for g in range(N_GROUPS)], axis=1) for ci in range(n_chunks)]
        y_b = rows[0] if n_chunks == 1 else jnp.concatenate(rows, axis=0)
        y_ref[b] = y_b[:tt].astype(BF16)

    @pl.when(ti == pl.num_programs(1) - 1)
    def _():
        s_out_ref[...] = s_scr[...]


def _wkv(r, k, v, kk, beta, lam, s0):
    b, t, _ = r.shape
    nb = WKV_BATCH if b % WKV_BATCH == 0 else 1
    tt = min(WKV_TILE, t)
    assert t % tt == 0
    c = CHUNK
    tri = np.tril(np.ones((c, c), np.float32))
    tri3 = jnp.asarray(np.concatenate([tri, tri, tri], axis=1), BF16)
    ones = _head_block_ones(GROUP)
    eye4 = jnp.asarray(np.tile(np.eye(c, dtype=np.float32), (1, HEADS_PER_GROUP)))
    tok = pl.BlockSpec((nb, tt, RWKV_WIDTH), lambda bi, i: (bi, i, 0))
    state = pl.BlockSpec((nb, N_GROUPS, GROUP, GROUP), lambda bi, i: (bi, 0, 0, 0))
    return pl.pallas_call(
        functools.partial(_wkv_kernel, nb=nb, tt=tt),
        grid=(b // nb, t // tt),
        in_specs=[tok, tok, tok, tok, tok, tok, state,
                  _const_spec((c, 3 * c)), _const_spec((GROUP, GROUP)), _const_spec((GROUP, GROUP)),
                  _const_spec((c, GROUP))],
        out_specs=[tok, state],
        out_shape=[jax.ShapeDtypeStruct((b, t, RWKV_WIDTH), BF16),
                   jax.ShapeDtypeStruct((b, N_GROUPS, GROUP, GROUP), F32)],
        scratch_shapes=[pltpu.VMEM((nb, N_GROUPS, GROUP, GROUP), F32)],
        compiler_params=pltpu.CompilerParams(dimension_semantics=("arbitrary", "arbitrary"),
                                             vmem_limit_bytes=VMEM_LIMIT_BYTES),
        name="wkv",
    )(r, k, v, kk, beta, lam, s0, tri3, jnp.asarray(ones, BF16), jnp.asarray(ones), eye4)


def _embed_states(s):
    b = s.shape[0]
    eye = jnp.eye(HEADS_PER_GROUP, dtype=s.dtype)
    s = s.reshape(b, N_GROUPS, HEADS_PER_GROUP, HEAD_DIM, HEAD_DIM)
    bd = s[:, :, :, :, None, :] * eye[None, None, :, None, :, None]
    return bd.reshape(b, N_GROUPS, GROUP, GROUP)


def _extract_states(bd):
    b = bd.shape[0]
    x = bd.reshape(b, N_GROUPS, HEADS_PER_GROUP, HEAD_DIM, HEADS_PER_GROUP, HEAD_DIM)
    blocks = [x[:, :, h, :, h, :] for h in range(HEADS_PER_GROUP)]
    return jnp.stack(blocks, axis=2).reshape(b, N_HEADS, HEAD_DIM, HEAD_DIM)


def _heads_first(x):
    b, t, _ = x.shape
    return x.reshape(b, t, N_HEADS, HEAD_DIM).transpose(0, 2, 1, 3)


def kernel(x_prompt, x_sample, state_shift, state_wkv, cache_attn_k, cache_attn_v, norm_ff1, w_ff1_gate, w_ff1_up, w_ff1_down, norm_mix, w_in, mu_shift, w0, w_lora_up, a0, a_lora_up, g_lora_up, k_k, k_a, r_k, ln_x_w, ln_x_b, rel_bias, w_out, norm_ff2, w_ff2_gate, w_ff2_up, w_ff2_down, norm_final):
    assert norm_ff1.shape[0] == 1, "single layer"
    bp, tp, _ = x_prompt.shape
    bs, ts, _ = x_sample.shape
    assert tp % ATT_WINDOW == 0 and ts <= CHUNK and cache_attn_k.shape[3] == ATT_WINDOW

    row = lambda w: w[0].reshape(1, -1).astype(F32)
    bf = lambda w: w[0].astype(BF16)
    ffn1 = (row(norm_ff1), bf(w_ff1_gate), bf(w_ff1_up), bf(w_ff1_down))
    ffn2 = (row(norm_ff2), bf(w_ff2_gate), bf(w_ff2_up), bf(w_ff2_down))
    w_out_b = bf(w_out)
    norm_final_row = norm_final.reshape(1, D_MODEL).astype(F32)
    lora_pad = jnp.zeros((HEAD_DIM, RWKV_WIDTH), BF16)
    gates = (row(mu_shift), row(w0), jnp.concatenate([bf(w_lora_up), lora_pad], axis=0), row(a0),
             jnp.concatenate([lora_pad, bf(a_lora_up)], axis=0), bf(g_lora_up), row(k_k), row(k_a), row(r_k))
    ones = jnp.asarray(_head_block_ones(RWKV_WIDTH), BF16)

    def mix(x, first, s0, t_seq):
        b, t, _ = x.shape
        x1, pa, r, k, v, kk, beta, lam, g, bonus, plast = _ffn_inproj(
            x.reshape(b * t, D_MODEL), first, ffn1, row(norm_mix), bf(w_in), gates, ones, t_seq)
        seq = lambda z: z.reshape(b, t, RWKV_WIDTH)
        y, s_bd = _wkv(seq(r), seq(k), seq(v), seq(kk), seq(beta), seq(lam), s0)
        return x1, pa.reshape(b, t, ATT_COLS), y, bonus, g, plast, s_bd

    def back(x1, y, bonus, g, at, b, t):
        out = _out_ffn(x1, y.reshape(b * t, RWKV_WIDTH), bonus, g, at.reshape(b * t, ATT_WIDTH),
                       row(ln_x_w), row(ln_x_b), ones, w_out_b[:RWKV_WIDTH], w_out_b[RWKV_WIDTH:],
                       ffn2, norm_final_row)
        return out.reshape(b, t, D_MODEL)

    x1, pa, y, bonus, g, plast, s_bd = mix(x_prompt, jnp.zeros((bp, 1, RWKV_COLS), F32),
                                           jnp.zeros((bp, N_GROUPS, GROUP, GROUP), F32), tp)
    at = _band_attn(pa, _rel_bias(rel_bias[0], CHUNK), ATT_WINDOW)
    y_prompt = back(x1, y, bonus, g, at, bp, tp)
    p_shift = plast.reshape(bp, -1, RWKV_COLS)[:, -1, :][None]
    p_wkv = _extract_states(s_bd)[None]
    p_k = _heads_first(pa[:, tp - ATT_WINDOW:, ATT_WIDTH:2 * ATT_WIDTH].astype(F32))[None]
    p_v = _heads_first(pa[:, tp - ATT_WINDOW:, 2 * ATT_WIDTH:].astype(F32))[None]

    x1, pa, y, bonus, g, plast, s_bd = mix(x_sample, jnp.repeat(state_shift[0], ts, axis=0),
                                           _embed_states(state_wkv[0]), ts)
    q = _heads_first(pa[:, :, :ATT_WIDTH].astype(F32))
    s_k = _heads_first(pa[:, :, ATT_WIDTH:2 * ATT_WIDTH].astype(F32))
    s_v = _heads_first(pa[:, :, 2 * ATT_WIDTH:].astype(F32))
    at = _step_attn(q, cache_attn_k[0], cache_attn_v[0], s_k, s_v, _rel_bias(rel_bias[0], ts))
    at = at.transpose(0, 2, 1, 3).reshape(bs, ts, ATT_WIDTH).astype(BF16)
    y_sample = back(x1, y, bonus, g, at, bs, ts)
    s_shift = plast.reshape(bs, ts, RWKV_COLS)[:, -1, :][None]
    s_wkv = _extract_states(s_bd)[None]

    return (y_prompt, y_sample, p_shift, p_wkv, p_k, p_v, s_shift, s_wkv, s_k[None], s_v[None])
```

```python
import functools
import math

import numpy as np
import jax
import jax.numpy as jnp
from jax import lax
from jax.experimental import pallas as pl
from jax.experimental.pallas import tpu as pltpu

F32 = jnp.float32
BF16 = jnp.bfloat16

D_MODEL = 1024
HEAD_DIM = 64
RWKV_WIDTH = 512
ATT_WIDTH = 512
N_HEADS = 8
LORA_COLS = 256
RWKV_COLS = 3 * RWKV_WIDTH + LORA_COLS
ATT_COLS = 3 * ATT_WIDTH
D_FF = 2816
CHUNK = 64
ATT_WINDOW = 512
REL_MAX = 128
NORM_EPS = 1e-5
GN_EPS = 64e-5
NEG_INF = -1e30

GROUP = 256
HEADS_PER_GROUP = GROUP // HEAD_DIM
N_GROUPS = RWKV_WIDTH // GROUP
KEY_PAD = ATT_WINDOW + 2 * CHUNK
LOG2_E = math.log2(math.e)
QK_SCALE = HEAD_DIM ** -0.5 * LOG2_E

TOKEN_TILE = 512
ROW_PART = 256
WKV_BATCH = 4
WKV_TILE = 128
VMEM_LIMIT_BYTES = 56 * 1024 * 1024


def _rms(x, g):
    ms = jnp.mean(x * x, axis=-1, keepdims=True)
    return x * lax.rsqrt(ms + NORM_EPS) * g


def _dot(a, b):
    return jnp.dot(a, b, preferred_element_type=F32)


def _dot_nt(a, b):
    return lax.dot_general(a, b, (((1,), (1,)), ((), ())), preferred_element_type=F32)


def _row_parts(tm):
    n = max(1, tm // ROW_PART)
    return [slice(i * (tm // n), (i + 1) * (tm // n)) for i in range(n)]


def _head_sums(x, ones_ref):
    return jnp.concatenate([_dot(x[:, g * GROUP:(g + 1) * GROUP], ones_ref[...]) for g in range(N_GROUPS)],
                           axis=1)


def _const_spec(shape):
    return pl.BlockSpec(shape, lambda *_: (0,) * len(shape), pipeline_mode=pl.Buffered(1))


def _head_block_ones(n):
    h = np.arange(n) // HEAD_DIM
    return (h[:, None] == h[None, :]).astype(np.float32)


def _ffn_inproj_kernel(x_ref, first_ref, g1_ref, wg_ref, wu_ref, wd_ref, gm_ref, win_ref,
                       mu_ref, w0_ref, wl_ref, a0_ref, al_ref, gl_ref, kkw_ref, ka_ref, rk_ref, ones_ref,
                       x1_ref, pa_ref, r_ref, k_ref, v_ref, kk_ref, beta_ref, lam_ref, g_ref, bonus_ref,
                       plast_ref, carry_scr, *, tm, t_seq):
    parts = _row_parts(tm)
    rows = parts[0].stop
    q_cols = slice(RWKV_COLS, RWKV_COLS + ATT_WIDTH)

    def gate_up(rs):
        h = _rms(x_ref[rs, :], g1_ref[...]).astype(BF16)
        return _dot(h, wg_ref[...]), _dot(h, wu_ref[...])

    def down_inproj(rs, gate, up):
        act = (gate * jax.nn.sigmoid(gate) * up).astype(BF16)
        x1 = x_ref[rs, :] + 0.5 * _dot(act, wd_ref[...])
        x1_ref[rs, :] = x1
        h2 = _rms(x1, gm_ref[...]).astype(BF16)
        p = _dot(h2, win_ref[:, :RWKV_COLS])
        pa_ref[rs, :ATT_WIDTH] = (_dot(h2, win_ref[:, q_cols]) * QK_SCALE).astype(BF16)
        pa_ref[rs, ATT_WIDTH:] = _dot(h2, win_ref[:, RWKV_COLS + ATT_WIDTH:]).astype(BF16)
        return p

    def gates(rs, p, before):
        row = lax.broadcasted_iota(jnp.int32, (rows, RWKV_COLS), 0)
        first_row = (row == 0) if t_seq >= tm else ((row & (t_seq - 1)) == 0)
        xs = p + (jnp.where(first_row, before, pltpu.roll(p, 1, axis=0)) - p) * mu_ref[...]
        r = xs[:, 0:RWKV_WIDTH]
        k = xs[:, RWKV_WIDTH:2 * RWKV_WIDTH]
        v = xs[:, 2 * RWKV_WIDTH:3 * RWKV_WIDTH]
        lora_in = xs[:, 3 * RWKV_WIDTH:3 * RWKV_WIDTH + 128]
        gate_in = xs[:, 3 * RWKV_WIDTH + 128:]
        z = w0_ref[...] + _dot(jnp.tanh(lora_in).astype(BF16), wl_ref[...])
        a = jax.nn.sigmoid(a0_ref[...] + _dot(lora_in.astype(BF16), al_ref[...]))
        g_ref[rs, :] = _dot(jax.nn.sigmoid(gate_in).astype(BF16), gl_ref[...]).astype(BF16)
        kk = k * kkw_ref[...]
        kk = kk * lax.rsqrt(jnp.maximum(_head_sums((kk * kk).astype(BF16), ones_ref), 1e-24))
        k = k * (1.0 + (a - 1.0) * ka_ref[...])
        lam_ref[rs, :] = -math.exp(-0.5) * jax.nn.sigmoid(z)
        r_ref[rs, :] = r.astype(BF16)
        k_ref[rs, :] = k.astype(BF16)
        v_ref[rs, :] = v.astype(BF16)
        kk_ref[rs, :] = kk.astype(BF16)
        beta_ref[rs, :] = (a * kk).astype(BF16)
        bonus_ref[rs, :] = (_head_sums((r * k * rk_ref[...]).astype(BF16), ones_ref) * v).astype(BF16)

    if t_seq >= tm:
        @pl.when(pl.program_id(0) % (t_seq // tm) == 0)
        def _():
            carry_scr[...] = first_ref[0]
        before = carry_scr[...]
    else:
        assert len(parts) == 1
        before = first_ref[...]

    gu = gate_up(parts[0])
    for i, rs in enumerate(parts):
        p = down_inproj(rs, *gu)
        if i + 1 < len(parts):
            gu = gate_up(parts[i + 1])
        gates(rs, p, before)
        before = p[rows - 1:rows, :]
    if t_seq >= tm:
        carry_scr[...] = before
        plast_ref[0] = before
    else:
        plast_ref[...] = p


def _ffn_inproj(x, first, ffn, gm, win, gates, ones, t_seq):
    n = x.shape[0]
    tm = min(TOKEN_TILE, n)
    assert n % tm == 0 and (t_seq % tm == 0 or tm % t_seq == 0) and t_seq & (t_seq - 1) == 0
    row = lambda i: (i, 0)
    vec = lambda m: _const_spec((1, m))
    wide = lambda dt: jax.ShapeDtypeStruct((n, RWKV_WIDTH), dt)
    wide_spec = pl.BlockSpec((tm, RWKV_WIDTH), row)
    if t_seq >= tm:
        first_spec = pl.BlockSpec((1, 1, RWKV_COLS), lambda i: (i // (t_seq // tm), 0, 0))
        plast_shape = jax.ShapeDtypeStruct((n // tm, 1, RWKV_COLS), F32)
        plast_spec = pl.BlockSpec((1, 1, RWKV_COLS), lambda i: (i, 0, 0))
    else:
        first_spec = pl.BlockSpec((tm, RWKV_COLS), row)
        plast_shape = jax.ShapeDtypeStruct((n, RWKV_COLS), F32)
        plast_spec = pl.BlockSpec((tm, RWKV_COLS), row)
    return pl.pallas_call(
        functools.partial(_ffn_inproj_kernel, tm=tm, t_seq=t_seq),
        grid=(n // tm,),
        in_specs=[pl.BlockSpec((tm, D_MODEL), row), first_spec,
                  vec(D_MODEL), _const_spec((D_MODEL, D_FF)), _const_spec((D_MODEL, D_FF)),
                  _const_spec((D_FF, D_MODEL)), vec(D_MODEL), _const_spec((D_MODEL, RWKV_COLS + ATT_COLS)),
                  vec(RWKV_COLS), vec(RWKV_WIDTH), _const_spec((128, RWKV_WIDTH)), vec(RWKV_WIDTH),
                  _const_spec((128, RWKV_WIDTH)), _const_spec((128, RWKV_WIDTH)),
                  vec(RWKV_WIDTH), vec(RWKV_WIDTH), vec(RWKV_WIDTH), _const_spec((GROUP, GROUP))],
        out_specs=[pl.BlockSpec((tm, D_MODEL), row), pl.BlockSpec((tm, ATT_COLS), row)]
                  + [wide_spec] * 8 + [plast_spec],
        out_shape=[jax.ShapeDtypeStruct((n, D_MODEL), F32), jax.ShapeDtypeStruct((n, ATT_COLS), BF16),
                   wide(BF16), wide(BF16), wide(BF16), wide(BF16), wide(BF16), wide(F32), wide(BF16),
                   wide(BF16), plast_shape],
        scratch_shapes=[pltpu.VMEM((1, RWKV_COLS), F32)],
        compiler_params=pltpu.CompilerParams(dimension_semantics=("arbitrary",),
                                             vmem_limit_bytes=VMEM_LIMIT_BYTES),
        name="ffn_inproj",
    )(x, first, *ffn, gm, win, *gates, ones)


def _out_ffn_kernel(x1_ref, y_ref, bonus_ref, g_ref, at_ref, lnw_ref, lnb_ref, ones_ref, wor_ref, woa_ref,
                    g2_ref, wg_ref, wu_ref, wd_ref, gf_ref, o_ref):
    parts = _row_parts(x1_ref.shape[0])
    nparts = range(len(parts))
    inv_n = 1.0 / HEAD_DIM
    mean = [_head_sums(y_ref[rs, :], ones_ref) * inv_n for rs in parts]
    d = [y_ref[parts[i], :].astype(F32) - mean[i] for i in nparts]
    var = [_head_sums((t * t).astype(BF16), ones_ref) * inv_n for t in d]
    ro = []
    for i in nparts:
        rs = parts[i]
        yn = d[i] * lax.rsqrt(var[i] + GN_EPS) * lnw_ref[...] + lnb_ref[...]
        ro.append(((yn + bonus_ref[rs, :].astype(F32)) * g_ref[rs, :].astype(F32)).astype(BF16))
    x2 = [x1_ref[parts[i], :] + _dot(ro[i], wor_ref[...]) + _dot(at_ref[parts[i], :], woa_ref[...])
          for i in nparts]
    h = [_rms(t, g2_ref[...]).astype(BF16) for t in x2]
    gu = [(_dot(hi, wg_ref[...]), _dot(hi, wu_ref[...])) for hi in h]
    act = [(gate * jax.nn.sigmoid(gate) * up).astype(BF16) for gate, up in gu]
    x3 = [x2[i] + 0.5 * _dot(act[i], wd_ref[...]) for i in nparts]
    for i in nparts:
        o_ref[parts[i], :] = _rms(x3[i], gf_ref[...])


def _out_ffn(x1, y, bonus, g, at, lnw, lnb, ones, wor, woa, ffn, gf):
    n = x1.shape[0]
    tm = min(TOKEN_TILE, n)
    row = lambda i: (i, 0)
    vec = lambda m: _const_spec((1, m))
    wide_spec = pl.BlockSpec((tm, RWKV_WIDTH), row)
    return pl.pallas_call(
        _out_ffn_kernel,
        grid=(n // tm,),
        in_specs=[pl.BlockSpec((tm, D_MODEL), row), wide_spec, wide_spec, wide_spec, wide_spec,
                  vec(RWKV_WIDTH), vec(RWKV_WIDTH), _const_spec((GROUP, GROUP)),
                  _const_spec((RWKV_WIDTH, D_MODEL)), _const_spec((ATT_WIDTH, D_MODEL)),
                  vec(D_MODEL), _const_spec((D_MODEL, D_FF)), _const_spec((D_MODEL, D_FF)),
                  _const_spec((D_FF, D_MODEL)), vec(D_MODEL)],
        out_specs=pl.BlockSpec((tm, D_MODEL), row),
        out_shape=jax.ShapeDtypeStruct((n, D_MODEL), F32),
        compiler_params=pltpu.CompilerParams(dimension_semantics=("arbitrary",),
                                             vmem_limit_bytes=VMEM_LIMIT_BYTES),
        name="out_ffn",
    )(x1, y, bonus, g, at, lnw, lnb, ones, wor, woa, *ffn, gf)


def _bias_kernel(tab_ref, o_ref, *, cq):
    qi = lax.broadcasted_iota(jnp.int32, (cq, KEY_PAD), 0)
    kj = lax.broadcasted_iota(jnp.int32, (cq, KEY_PAD), 1)
    idx = jnp.clip(kj - ATT_WINDOW - qi, -REL_MAX, REL_MAX) + REL_MAX
    pad = kj >= ATT_WINDOW + cq
    for h in range(N_HEADS):
        def body(t, acc):
            return jnp.where(idx == t, tab_ref[h, t], acc)
        acc = lax.fori_loop(0, 2 * REL_MAX + 1, body, jnp.zeros((cq, KEY_PAD), F32))
        o_ref[h * cq:(h + 1) * cq, :] = jnp.where(pad, NEG_INF, acc * LOG2_E)


def _rel_bias(table, cq):
    return pl.pallas_call(
        functools.partial(_bias_kernel, cq=cq),
        in_specs=[pl.BlockSpec(memory_space=pltpu.SMEM)],
        out_specs=pl.BlockSpec(memory_space=pltpu.VMEM),
        out_shape=jax.ShapeDtypeStruct((N_HEADS * cq, KEY_PAD), F32),
        name=f"rel_bias_{cq}",
    )(table)


def _head_masks(rows, dtype):
    lane_head = lax.broadcasted_iota(jnp.int32, (rows, GROUP), 1) >> 6
    return [(lane_head == h).astype(F32).astype(dtype) for h in range(HEADS_PER_GROUP)]


def _block_diag_rows(x, masks):
    return jnp.concatenate([x * m for m in masks], axis=0)


def _band_attn_kernel(q_ref, kp_ref, kc_ref, vp_ref, vc_ref, bias_ref, o_ref, kbuf, vbuf, *, tq):
    kbuf[0:tq, :] = kp_ref[0]
    kbuf[tq:2 * tq, :] = kc_ref[0]
    kbuf[2 * tq:, :] = jnp.zeros((CHUNK, ATT_WIDTH), BF16)
    vbuf[0:tq, :] = vp_ref[0]
    vbuf[tq:2 * tq, :] = vc_ref[0]
    vbuf[2 * tq:, :] = jnp.zeros((CHUNK, ATT_WIDTH), BF16)
    masks_b = _head_masks(CHUNK, BF16)
    masks_f = _head_masks(CHUNK, F32)
    rows = HEADS_PER_GROUP * CHUNK
    cells = [(j, g) for j in range(tq // CHUNK) for g in range(N_GROUPS)]
    lanes = [slice(g * GROUP, (g + 1) * GROUP) for g in range(N_GROUPS)]
    win = [slice(j * CHUNK + tq - ATT_WINDOW, j * CHUNK + tq - ATT_WINDOW + KEY_PAD)
           for j in range(tq // CHUNK)]

    def run(first_tile):
        def scores(j, g):
            lhs = _block_diag_rows(q_ref[0, j * CHUNK:(j + 1) * CHUNK, lanes[g]], masks_b)
            return _dot_nt(lhs, kbuf[win[j], lanes[g]])

        def softmax(j, g, s):
            s = s + bias_ref[g * rows:(g + 1) * rows, :]
            if first_tile:
                kj = lax.broadcasted_iota(jnp.int32, (rows, KEY_PAD), 1)
                s = jnp.where(kj >= tq - j * CHUNK, s, NEG_INF)
            p = jnp.exp2(s - jnp.max(s, axis=-1, keepdims=True))
            return p.astype(BF16), 1.0 / jnp.sum(p, axis=-1, keepdims=True)

        def values(j, g, p, inv):
            o_full = _dot(p, vbuf[win[j], lanes[g]]) * inv
            o = o_full[0:CHUNK] * masks_f[0]
            for h in range(1, HEADS_PER_GROUP):
                o = o + o_full[h * CHUNK:(h + 1) * CHUNK] * masks_f[h]
            return o

        s, pz, o = {}, {}, {}
        for n in range(len(cells) + 2):
            if n < len(cells):
                s[n] = scores(*cells[n])
            if 1 <= n <= len(cells):
                pz[n - 1] = softmax(*cells[n - 1], s.pop(n - 1))
            if n >= 2:
                j, g = cells[n - 2]
                o[g] = values(j, g, *pz.pop(n - 2))
                if g == N_GROUPS - 1:
                    o_ref[0, j * CHUNK:(j + 1) * CHUNK, :] = jnp.concatenate(
                        [o[gg] for gg in range(N_GROUPS)], axis=1).astype(BF16)

    @pl.when(pl.program_id(1) == 0)
    def _():
        run(True)

    @pl.when(pl.program_id(1) > 0)
    def _():
        run(False)


def _band_attn(p_att, bias, tq):
    b, t, _ = p_att.shape
    assert tq == ATT_WINDOW and t % tq == 0
    cur = lambda col: (lambda bi, i: (bi, i, col))
    prev = lambda col: (lambda bi, i: (bi, jnp.maximum(i - 1, 0), col))
    blk = (1, tq, ATT_WIDTH)
    return pl.pallas_call(
        functools.partial(_band_attn_kernel, tq=tq),
        grid=(b, t // tq),
        in_specs=[pl.BlockSpec(blk, cur(0)), pl.BlockSpec(blk, prev(1)), pl.BlockSpec(blk, cur(1)),
                  pl.BlockSpec(blk, prev(2)), pl.BlockSpec(blk, cur(2)),
                  _const_spec((N_HEADS * CHUNK, KEY_PAD))],
        out_specs=pl.BlockSpec(blk, cur(0)),
        out_shape=jax.ShapeDtypeStruct((b, t, ATT_WIDTH), BF16),
        scratch_shapes=[pltpu.VMEM((2 * tq + CHUNK, ATT_WIDTH), BF16),
                        pltpu.VMEM((2 * tq + CHUNK, ATT_WIDTH), BF16)],
        compiler_params=pltpu.CompilerParams(dimension_semantics=("arbitrary", "arbitrary"),
                                             vmem_limit_bytes=VMEM_LIMIT_BYTES),
        name="band_attn",
    )(p_att, p_att, p_att, p_att, p_att, bias)


def _step_attn_kernel(q_ref, kc_ref, vc_ref, kn_ref, vn_ref, bias_ref, o_ref, *, tn):
    for h in range(N_HEADS):
        q = q_ref[0, h].astype(BF16)
        b = bias_ref[h * tn:(h + 1) * tn, :]
        s1 = _dot_nt(q, kc_ref[0, h].astype(BF16)) + b[:, :ATT_WINDOW]
        s2 = _dot_nt(q, kn_ref[0, h].astype(BF16)) + b[:, ATT_WINDOW:ATT_WINDOW + tn]
        m = jnp.maximum(jnp.max(s1, axis=-1, keepdims=True), jnp.max(s2, axis=-1, keepdims=True))
        p1 = jnp.exp2(s1 - m)
        p2 = jnp.exp2(s2 - m)
        inv = 1.0 / (jnp.sum(p1, axis=-1, keepdims=True) + jnp.sum(p2, axis=-1, keepdims=True))
        o = _dot(p1.astype(BF16), vc_ref[0, h].astype(BF16)) + _dot(p2.astype(BF16), vn_ref[0, h].astype(BF16))
        o_ref[0, h] = o * inv


def _step_attn(q, k_cache, v_cache, k_new, v_new, bias):
    b, _, tn, _ = q.shape
    new = pl.BlockSpec((1, N_HEADS, tn, HEAD_DIM), lambda i: (i, 0, 0, 0))
    old = pl.BlockSpec((1, N_HEADS, ATT_WINDOW, HEAD_DIM), lambda i: (i, 0, 0, 0))
    return pl.pallas_call(
        functools.partial(_step_attn_kernel, tn=tn),
        grid=(b,),
        in_specs=[new, old, old, new, new, _const_spec((N_HEADS * tn, KEY_PAD))],
        out_specs=new,
        out_shape=jax.ShapeDtypeStruct((b, N_HEADS, tn, HEAD_DIM), F32),
        compiler_params=pltpu.CompilerParams(dimension_semantics=("arbitrary",)),
        name="step_attn",
    )(q, k_cache, v_cache, k_new, v_new, bias)


def _split3(x):
    hi = x.astype(BF16)
    r1 = x - hi.astype(F32)
    mid = r1.astype(BF16)
    lo = (r1 - mid.astype(F32)).astype(BF16)
    return hi, mid, lo


def _wkv_kernel(r_ref, k_ref, v_ref, kk_ref, beta_ref, lam_ref, s0_ref, tri3_ref, hm_ref, bdm_ref, eye4_ref,
                y_ref, s_out_ref, s_scr, *, nb, tt):
    ti = pl.program_id(1)

    @pl.when(ti == 0)
    def _():
        s_scr[...] = s0_ref[...]

    c = CHUNK
    t_row = lax.broadcasted_iota(jnp.int32, (c, GROUP), 0)
    s_col = lax.broadcasted_iota(jnp.int32, (c, GROUP), 1) & (c - 1)
    strict = s_col < t_row
    incl = s_col <= t_row
    tri3, hm, bdm, eye4 = tri3_ref[...], hm_ref[...], bdm_ref[...], eye4_ref[...]
    zero = jnp.zeros((), F32)
    n_chunks = -(-tt // c)
    pad = n_chunks * c - tt

    def bd(x):
        return jnp.concatenate([x] * HEADS_PER_GROUP, axis=0) * hm

    def padded(x):
        return x if pad == 0 else jnp.concatenate([x, jnp.zeros((pad, x.shape[1]), x.dtype)], axis=0)

    lanes_of = [(b, g) for b in range(nb) for g in range(N_GROUPS)]
    cells = [(b, g, ci) for ci in range(n_chunks) for (b, g) in lanes_of]

    def cut(ref, cell, cast):
        b, g, ci = cell
        lanes = slice(g * GROUP, (g + 1) * GROUP)
        x = ref[b, ci * c:(ci + 1) * c, lanes] if pad == 0 else padded(ref[b, :, lanes])
        return x.astype(F32) if cast else x

    lam = {z: cut(lam_ref, z, False) for z in cells}
    cum = {z: _dot(tri3, jnp.concatenate(_split3(lam[z]), axis=0)) for z in cells}
    tot = {z: cum[z][c - 1:c, :] for z in cells}
    lhs, bt, kt, bk, v_bd = {}, {}, {}, {}, {}
    for z in cells:
        r, k, kk, beta = (cut(ref, z, True) for ref in (r_ref, k_ref, kk_ref, beta_ref))
        e_neg = jnp.exp(-cum[z])
        e_rem = jnp.exp(tot[z] - cum[z])
        lhs[z] = jnp.concatenate([-kk * jnp.exp(cum[z] - lam[z]), r * jnp.exp(cum[z])],
                                 axis=0).astype(BF16)
        bt[z] = bd((beta * e_neg).astype(BF16))
        kt[z] = bd((k * e_neg).astype(BF16))
        bk[z] = jnp.concatenate([beta * e_rem, k * e_rem], axis=0).astype(BF16)
        v_bd[z] = bd(cut(v_ref, z, False))
    ab = {z: _dot_nt(lhs[z], bt[z]) for z in cells}
    ak = {z: _dot_nt(lhs[z], kt[z]) for z in cells}
    a_rb = {z: jnp.where(incl, ab[z][c:], zero).astype(BF16) for z in cells}
    a_ak = {z: jnp.where(strict, ak[z][:c], zero).astype(BF16) for z in cells}
    a_rk = {z: jnp.where(incl, ak[z][c:], zero).astype(BF16) for z in cells}
    n_k = {z: jnp.where(strict, ab[z][:c], zero) for z in cells}
    n_b = {z: n_k[z].astype(BF16) for z in cells}
    t_m = {z: eye4 + n_k[z] for z in cells}
    n_b = {z: _dot(n_b[z], bd(n_b[z])).astype(BF16) for z in cells}
    for _ in range(int(math.log2(c)) - 2):
        both = {z: _dot(jnp.concatenate([t_m[z].astype(BF16), n_b[z]], axis=0), bd(n_b[z])) for z in cells}
        t_m = {z: t_m[z] + both[z][:c] for z in cells}
        n_b = {z: both[z][c:].astype(BF16) for z in cells}
    t_b = {z: (t_m[z] + _dot(t_m[z].astype(BF16), bd(n_b[z]))).astype(BF16) for z in cells}
    wy_v = {z: _dot(jnp.concatenate([a_ak[z], a_rk[z]], axis=0), v_bd[z]) for z in cells}
    w_v = {z: wy_v[z][:c] for z in cells}
    y_v = {z: wy_v[z][c:] for z in cells}

    s_t = {bg: s_scr[bg[0], bg[1]] for bg in lanes_of}
    y = {}
    for ci in range(n_chunks):
        zs = [(b, g, ci) for (b, g) in lanes_of]
        x = {z: _dot_nt(lhs[z], s_t[z[:2]].astype(BF16)) for z in zs}
        u = {z: _dot(t_b[z], bd((x[z][:c] + w_v[z]).astype(BF16))) for z in zs}
        for z in zs:
            uv_t = jnp.concatenate([u[z], cut(v_ref, z, True)], axis=0).T.astype(BF16)
            s_t[z[:2]] = s_t[z[:2]] * jnp.exp(tot[z]) + _dot(uv_t, bk[z]) * bdm
        for z in zs:
            y[z] = x[z][c:] + y_v[z] + _dot(a_rb[z], bd(u[z].astype(BF16)))
    for (b, g) in lanes_of:
        s_scr[b, g] = s_t[(b, g)]
    for b in range(nb):
        rows = [jnp.concatenate([y[(b, g, ci)] for g in range(N_GROUPS)], axis=1) for ci in range(n_chunks)]
        y_b = rows[0] if n_chunks == 1 else jnp.concatenate(rows, axis=0)
        y_ref[b] = y_b[:tt].astype(BF16)

    @pl.when(ti == pl.num_programs(1) - 1)
    def _():
        s_out_ref[...] = s_scr[...]


def _wkv(r, k, v, kk, beta, lam, s0):
    b, t, _ = r.shape
    nb = WKV_BATCH if b % WKV_BATCH == 0 else 1
    tt = min(WKV_TILE, t)
    assert t % tt == 0
    c = CHUNK
    tri = np.tril(np.ones((c, c), np.float32))
    tri3 = jnp.asarray(np.concatenate([tri, tri, tri], axis=1), BF16)
    ones = _head_block_ones(GROUP)
    eye4 = jnp.asarray(np.tile(np.eye(c, dtype=np.float32), (1, HEADS_PER_GROUP)))
    tok = pl.BlockSpec((nb, tt, RWKV_WIDTH), lambda bi, i: (bi, i, 0))
    state = pl.BlockSpec((nb, N_GROUPS, GROUP, GROUP), lambda bi, i: (bi, 0, 0, 0))
    return pl.pallas_call(
        functools.partial(_wkv_kernel, nb=nb, tt=tt),
        grid=(b // nb, t // tt),
        in_specs=[tok, tok, tok, tok, tok, tok, state,
                  _const_spec((c, 3 * c)), _const_spec((GROUP, GROUP)), _const_spec((GROUP, GROUP)),
                  _const_spec((c, GROUP))],
        out_specs=[tok, state],
        out_shape=[jax.ShapeDtypeStruct((b, t, RWKV_WIDTH), BF16),
                   jax.ShapeDtypeStruct((b, N_GROUPS, GROUP, GROUP), F32)],
        scratch_shapes=[pltpu.VMEM((nb, N_GROUPS, GROUP, GROUP), F32)],
        compiler_params=pltpu.CompilerParams(dimension_semantics=("arbitrary", "arbitrary"),
                                             vmem_limit_bytes=VMEM_LIMIT_BYTES),
        name="wkv",
    )(r, k, v, kk, beta, lam, s0, tri3, jnp.asarray(ones, BF16), jnp.asarray(ones), eye4)


def _embed_states(s):
    b = s.shape[0]
    eye = jnp.eye(HEADS_PER_GROUP, dtype=s.dtype)
    s = s.reshape(b, N_GROUPS, HEADS_PER_GROUP, HEAD_DIM, HEAD_DIM)
    bd = s[:, :, :, :, None, :] * eye[None, None, :, None, :, None]
    return bd.reshape(b, N_GROUPS, GROUP, GROUP)


def _extract_states(bd):
    b = bd.shape[0]
    x = bd.reshape(b, N_GROUPS, HEADS_PER_GROUP, HEAD_DIM, HEADS_PER_GROUP, HEAD_DIM)
    blocks = [x[:, :, h, :, h, :] for h in range(HEADS_PER_GROUP)]
    return jnp.stack(blocks, axis=2).reshape(b, N_HEADS, HEAD_DIM, HEAD_DIM)


def _heads_first(x):
    b, t, _ = x.shape
    return x.reshape(b, t, N_HEADS, HEAD_DIM).transpose(0, 2, 1, 3)


def kernel(x_prompt, x_sample, state_shift, state_wkv, cache_attn_k, cache_attn_v, norm_ff1, w_ff1_gate, w_ff1_up, w_ff1_down, norm_mix, w_in, mu_shift, w0, w_lora_up, a0, a_lora_up, g_lora_up, k_k, k_a, r_k, ln_x_w, ln_x_b, rel_bias, w_out, norm_ff2, w_ff2_gate, w_ff2_up, w_ff2_down, norm_final):
    assert norm_ff1.shape[0] == 1, "single layer"
    bp, tp, _ = x_prompt.shape
    bs, ts, _ = x_sample.shape
    assert tp % ATT_WINDOW == 0 and ts <= CHUNK and cache_attn_k.shape[3] == ATT_WINDOW

    row = lambda w: w[0].reshape(1, -1).astype(F32)
    bf = lambda w: w[0].astype(BF16)
    ffn1 = (row(norm_ff1), bf(w_ff1_gate), bf(w_ff1_up), bf(w_ff1_down))
    ffn2 = (row(norm_ff2), bf(w_ff2_gate), bf(w_ff2_up), bf(w_ff2_down))
    w_out_b = bf(w_out)
    norm_final_row = norm_final.reshape(1, D_MODEL).astype(F32)
    lora_pad = jnp.zeros((HEAD_DIM, RWKV_WIDTH), BF16)
    gates = (row(mu_shift), row(w0), jnp.concatenate([bf(w_lora_up), lora_pad], axis=0), row(a0),
             jnp.concatenate([lora_pad, bf(a_lora_up)], axis=0), bf(g_lora_up), row(k_k), row(k_a), row(r_k))
    ones = jnp.asarray(_head_block_ones(GROUP), BF16)

    def mix(x, first, s0, t_seq):
        b, t, _ = x.shape
        x1, pa, r, k, v, kk, beta, lam, g, bonus, plast = _ffn_inproj(
            x.reshape(b * t, D_MODEL), first, ffn1, row(norm_mix), bf(w_in), gates, ones, t_seq)
        seq = lambda z: z.reshape(b, t, RWKV_WIDTH)
        y, s_bd = _wkv(seq(r), seq(k), seq(v), seq(kk), seq(beta), seq(lam), s0)
        return x1, pa.reshape(b, t, ATT_COLS), y, bonus, g, plast, s_bd

    def back(x1, y, bonus, g, at, b, t):
        out = _out_ffn(x1, y.reshape(b * t, RWKV_WIDTH), bonus, g, at.reshape(b * t, ATT_WIDTH),
                       row(ln_x_w), row(ln_x_b), ones, w_out_b[:RWKV_WIDTH], w_out_b[RWKV_WIDTH:],
                       ffn2, norm_final_row)
        return out.reshape(b, t, D_MODEL)

    x1, pa, y, bonus, g, plast, s_bd = mix(x_prompt, jnp.zeros((bp, 1, RWKV_COLS), F32),
                                           jnp.zeros((bp, N_GROUPS, GROUP, GROUP), F32), tp)
    at = _band_attn(pa, _rel_bias(rel_bias[0], CHUNK), ATT_WINDOW)
    y_prompt = back(x1, y, bonus, g, at, bp, tp)
    p_shift = plast.reshape(bp, -1, RWKV_COLS)[:, -1, :][None]
    p_wkv = _extract_states(s_bd)[None]
    p_k = _heads_first(pa[:, tp - ATT_WINDOW:, ATT_WIDTH:2 * ATT_WIDTH].astype(F32))[None]
    p_v = _heads_first(pa[:, tp - ATT_WINDOW:, 2 * ATT_WIDTH:].astype(F32))[None]

    x1, pa, y, bonus, g, plast, s_bd = mix(x_sample, jnp.repeat(state_shift[0], ts, axis=0),
                                           _embed_states(state_wkv[0]), ts)
    q = _heads_first(pa[:, :, :ATT_WIDTH].astype(F32))
    s_k = _heads_first(pa[:, :, ATT_WIDTH:2 * ATT_WIDTH].astype(F32))
    s_v = _heads_first(pa[:, :, 2 * ATT_WIDTH:].astype(F32))
    at = _step_attn(q, cache_attn_k[0], cache_attn_v[0], s_k, s_v, _rel_bias(rel_bias[0], ts))
    at = at.transpose(0, 2, 1, 3).reshape(bs, ts, ATT_WIDTH).astype(BF16)
    y_sample = back(x1, y, bonus, g, at, bs, ts)
    s_shift = plast.reshape(bs, ts, RWKV_COLS)[:, -1, :][None]
    s_wkv = _extract_states(s_bd)[None]

    return (y_prompt, y_sample, p_shift, p_wkv, p_k, p_v, s_shift, s_wkv, s_k[None], s_v[None])
```

```python
import functools
import math

import numpy as np
import jax
import jax.numpy as jnp
from jax import lax
from jax.experimental import pallas as pl
from jax.experimental.pallas import tpu as pltpu

F32 = jnp.float32
BF16 = jnp.bfloat16

D_MODEL = 1024
HEAD_DIM = 64
RWKV_WIDTH = 512
ATT_WIDTH = 512
N_HEADS = 8
LORA_COLS = 256
RWKV_COLS = 3 * RWKV_WIDTH + LORA_COLS
ATT_COLS = 3 * ATT_WIDTH
D_FF = 2816
CHUNK = 64
ATT_WINDOW = 512
REL_MAX = 128
NORM_EPS = 1e-5
GN_EPS = 64e-5
NEG_INF = -1e30

GROUP = 256
HEADS_PER_GROUP = GROUP // HEAD_DIM
N_GROUPS = RWKV_WIDTH // GROUP
KEY_PAD = ATT_WINDOW + 2 * CHUNK
LOG2_E = math.log2(math.e)
QK_SCALE = HEAD_DIM ** -0.5 * LOG2_E

TOKEN_TILE = 512
ROW_PART = 256
WKV_BATCH = 4
WKV_TILE = 128
FF_CHUNK = 256
FUSED_BURST = 8
FUSED_CHAIN_GAP = 2
VMEM_LIMIT_BYTES = 56 * 1024 * 1024
FUSED_VMEM_LIMIT_BYTES = 60 * 1024 * 1024


def _rms(x, g):
    ms = jnp.mean(x * x, axis=-1, keepdims=True)
    return x * lax.rsqrt(ms + NORM_EPS) * g


def _dot(a, b):
    return jnp.dot(a, b, preferred_element_type=F32)


def _dot_nt(a, b):
    return lax.dot_general(a, b, (((1,), (1,)), ((), ())), preferred_element_type=F32)


def _row_parts(tm):
    n = max(1, tm // ROW_PART)
    return [slice(i * (tm // n), (i + 1) * (tm // n)) for i in range(n)]


def _head_sums(x, ones_ref):
    return jnp.concatenate([_dot(x[:, g * GROUP:(g + 1) * GROUP], ones_ref[...]) for g in range(N_GROUPS)],
                           axis=1)


def _const_spec(shape):
    return pl.BlockSpec(shape, lambda *_: (0,) * len(shape), pipeline_mode=pl.Buffered(1))


def _head_block_ones(n):
    h = np.arange(n) // HEAD_DIM
    return (h[:, None] == h[None, :]).astype(np.float32)


def _ffn_inproj_kernel(x_ref, first_ref, g1_ref, wg_ref, wu_ref, wd_ref, gm_ref, win_ref,
                       mu_ref, w0_ref, wl_ref, a0_ref, al_ref, gl_ref, kkw_ref, ka_ref, rk_ref, ones_ref,
                       x1_ref, pa_ref, r_ref, k_ref, v_ref, kk_ref, beta_ref, lam_ref, g_ref, bonus_ref,
                       plast_ref, carry_scr, *, tm, t_seq):
    parts = _row_parts(tm)
    rows = parts[0].stop
    q_cols = slice(RWKV_COLS, RWKV_COLS + ATT_WIDTH)

    def gate_up(rs):
        h = _rms(x_ref[rs, :], g1_ref[...]).astype(BF16)
        return _dot(h, wg_ref[...]), _dot(h, wu_ref[...])

    def down_inproj(rs, gate, up):
        act = (gate * jax.nn.sigmoid(gate) * up).astype(BF16)
        x1 = x_ref[rs, :] + 0.5 * _dot(act, wd_ref[...])
        x1_ref[rs, :] = x1
        h2 = _rms(x1, gm_ref[...]).astype(BF16)
        p = _dot(h2, win_ref[:, :RWKV_COLS])
        pa_ref[rs, :ATT_WIDTH] = (_dot(h2, win_ref[:, q_cols]) * QK_SCALE).astype(BF16)
        pa_ref[rs, ATT_WIDTH:] = _dot(h2, win_ref[:, RWKV_COLS + ATT_WIDTH:]).astype(BF16)
        return p

    def gates(rs, p, before):
        row = lax.broadcasted_iota(jnp.int32, (rows, RWKV_COLS), 0)
        first_row = (row == 0) if t_seq >= tm else ((row & (t_seq - 1)) == 0)
        xs = p + (jnp.where(first_row, before, pltpu.roll(p, 1, axis=0)) - p) * mu_ref[...]
        r = xs[:, 0:RWKV_WIDTH]
        k = xs[:, RWKV_WIDTH:2 * RWKV_WIDTH]
        v = xs[:, 2 * RWKV_WIDTH:3 * RWKV_WIDTH]
        lora_in = xs[:, 3 * RWKV_WIDTH:3 * RWKV_WIDTH + 128]
        gate_in = xs[:, 3 * RWKV_WIDTH + 128:]
        z = w0_ref[...] + _dot(jnp.tanh(lora_in).astype(BF16), wl_ref[...])
        a = jax.nn.sigmoid(a0_ref[...] + _dot(lora_in.astype(BF16), al_ref[...]))
        g_ref[rs, :] = _dot(jax.nn.sigmoid(gate_in).astype(BF16), gl_ref[...]).astype(BF16)
        kk = k * kkw_ref[...]
        kk = kk * lax.rsqrt(jnp.maximum(_head_sums((kk * kk).astype(BF16), ones_ref), 1e-24))
        k = k * (1.0 + (a - 1.0) * ka_ref[...])
        lam_ref[rs, :] = -math.exp(-0.5) * jax.nn.sigmoid(z)
        r_ref[rs, :] = r.astype(BF16)
        k_ref[rs, :] = k.astype(BF16)
        v_ref[rs, :] = v.astype(BF16)
        kk_ref[rs, :] = kk.astype(BF16)
        beta_ref[rs, :] = (a * kk).astype(BF16)
        bonus_ref[rs, :] = (_head_sums((r * k * rk_ref[...]).astype(BF16), ones_ref) * v).astype(BF16)

    if t_seq >= tm:
        @pl.when(pl.program_id(0) % (t_seq // tm) == 0)
        def _():
            carry_scr[...] = first_ref[0]
        before = carry_scr[...]
    else:
        assert len(parts) == 1
        before = first_ref[...]

    gu = gate_up(parts[0])
    for i, rs in enumerate(parts):
        p = down_inproj(rs, *gu)
        if i + 1 < len(parts):
            gu = gate_up(parts[i + 1])
        gates(rs, p, before)
        before = p[rows - 1:rows, :]
    if t_seq >= tm:
        carry_scr[...] = before
        plast_ref[0] = before
    else:
        plast_ref[...] = p


def _ffn_inproj(x, first, ffn, gm, win, gates, ones, t_seq):
    n = x.shape[0]
    tm = min(TOKEN_TILE, n)
    assert n % tm == 0 and (t_seq % tm == 0 or tm % t_seq == 0) and t_seq & (t_seq - 1) == 0
    row = lambda i: (i, 0)
    vec = lambda m: _const_spec((1, m))
    wide = lambda dt: jax.ShapeDtypeStruct((n, RWKV_WIDTH), dt)
    wide_spec = pl.BlockSpec((tm, RWKV_WIDTH), row)
    if t_seq >= tm:
        first_spec = pl.BlockSpec((1, 1, RWKV_COLS), lambda i: (i // (t_seq // tm), 0, 0))
        plast_shape = jax.ShapeDtypeStruct((n // tm, 1, RWKV_COLS), F32)
        plast_spec = pl.BlockSpec((1, 1, RWKV_COLS), lambda i: (i, 0, 0))
    else:
        first_spec = pl.BlockSpec((tm, RWKV_COLS), row)
        plast_shape = jax.ShapeDtypeStruct((n, RWKV_COLS), F32)
        plast_spec = pl.BlockSpec((tm, RWKV_COLS), row)
    return pl.pallas_call(
        functools.partial(_ffn_inproj_kernel, tm=tm, t_seq=t_seq),
        grid=(n // tm,),
        in_specs=[pl.BlockSpec((tm, D_MODEL), row), first_spec,
                  vec(D_MODEL), _const_spec((D_MODEL, D_FF)), _const_spec((D_MODEL, D_FF)),
                  _const_spec((D_FF, D_MODEL)), vec(D_MODEL), _const_spec((D_MODEL, RWKV_COLS + ATT_COLS)),
                  vec(RWKV_COLS), vec(RWKV_WIDTH), _const_spec((128, RWKV_WIDTH)), vec(RWKV_WIDTH),
                  _const_spec((128, RWKV_WIDTH)), _const_spec((128, RWKV_WIDTH)),
                  vec(RWKV_WIDTH), vec(RWKV_WIDTH), vec(RWKV_WIDTH), _const_spec((GROUP, GROUP))],
        out_specs=[pl.BlockSpec((tm, D_MODEL), row), pl.BlockSpec((tm, ATT_COLS), row)]
                  + [wide_spec] * 8 + [plast_spec],
        out_shape=[jax.ShapeDtypeStruct((n, D_MODEL), F32), jax.ShapeDtypeStruct((n, ATT_COLS), BF16),
                   wide(BF16), wide(BF16), wide(BF16), wide(BF16), wide(BF16), wide(F32), wide(BF16),
                   wide(BF16), plast_shape],
        scratch_shapes=[pltpu.VMEM((1, RWKV_COLS), F32)],
        compiler_params=pltpu.CompilerParams(dimension_semantics=("arbitrary",),
                                             vmem_limit_bytes=VMEM_LIMIT_BYTES),
        name="ffn_inproj",
    )(x, first, *ffn, gm, win, *gates, ones)


def _out_ffn_kernel(x1_ref, y_ref, bonus_ref, g_ref, at_ref, lnw_ref, lnb_ref, ones_ref, wor_ref, woa_ref,
                    g2_ref, wg_ref, wu_ref, wd_ref, gf_ref, o_ref):
    parts = _row_parts(x1_ref.shape[0])
    nparts = range(len(parts))
    inv_n = 1.0 / HEAD_DIM
    mean = [_head_sums(y_ref[rs, :], ones_ref) * inv_n for rs in parts]
    d = [y_ref[parts[i], :].astype(F32) - mean[i] for i in nparts]
    var = [_head_sums((t * t).astype(BF16), ones_ref) * inv_n for t in d]
    ro = []
    for i in nparts:
        rs = parts[i]
        yn = d[i] * lax.rsqrt(var[i] + GN_EPS) * lnw_ref[...] + lnb_ref[...]
        ro.append(((yn + bonus_ref[rs, :].astype(F32)) * g_ref[rs, :].astype(F32)).astype(BF16))
    x2 = [x1_ref[parts[i], :] + _dot(ro[i], wor_ref[...]) + _dot(at_ref[parts[i], :], woa_ref[...])
          for i in nparts]
    h = [_rms(t, g2_ref[...]).astype(BF16) for t in x2]
    gu = [(_dot(hi, wg_ref[...]), _dot(hi, wu_ref[...])) for hi in h]
    act = [(gate * jax.nn.sigmoid(gate) * up).astype(BF16) for gate, up in gu]
    x3 = [x2[i] + 0.5 * _dot(act[i], wd_ref[...]) for i in nparts]
    for i in nparts:
        o_ref[parts[i], :] = _rms(x3[i], gf_ref[...])


def _out_ffn(x1, y, bonus, g, at, lnw, lnb, ones, wor, woa, ffn, gf):
    n = x1.shape[0]
    tm = min(TOKEN_TILE, n)
    row = lambda i: (i, 0)
    vec = lambda m: _const_spec((1, m))
    wide_spec = pl.BlockSpec((tm, RWKV_WIDTH), row)
    return pl.pallas_call(
        _out_ffn_kernel,
        grid=(n // tm,),
        in_specs=[pl.BlockSpec((tm, D_MODEL), row), wide_spec, wide_spec, wide_spec, wide_spec,
                  vec(RWKV_WIDTH), vec(RWKV_WIDTH), _const_spec((GROUP, GROUP)),
                  _const_spec((RWKV_WIDTH, D_MODEL)), _const_spec((ATT_WIDTH, D_MODEL)),
                  vec(D_MODEL), _const_spec((D_MODEL, D_FF)), _const_spec((D_MODEL, D_FF)),
                  _const_spec((D_FF, D_MODEL)), vec(D_MODEL)],
        out_specs=pl.BlockSpec((tm, D_MODEL), row),
        out_shape=jax.ShapeDtypeStruct((n, D_MODEL), F32),
        compiler_params=pltpu.CompilerParams(dimension_semantics=("arbitrary",),
                                             vmem_limit_bytes=VMEM_LIMIT_BYTES),
        name="out_ffn",
    )(x1, y, bonus, g, at, lnw, lnb, ones, wor, woa, *ffn, gf)


def _bias_kernel(tab_ref, o_ref, *, cq):
    qi = lax.broadcasted_iota(jnp.int32, (cq, KEY_PAD), 0)
    kj = lax.broadcasted_iota(jnp.int32, (cq, KEY_PAD), 1)
    idx = jnp.clip(kj - ATT_WINDOW - qi, -REL_MAX, REL_MAX) + REL_MAX
    pad = kj >= ATT_WINDOW + cq
    for h in range(N_HEADS):
        def body(t, acc):
            return jnp.where(idx == t, tab_ref[h, t], acc)
        acc = lax.fori_loop(0, REL_MAX + cq, body, jnp.zeros((cq, KEY_PAD), F32))
        o_ref[h * cq:(h + 1) * cq, :] = jnp.where(pad, NEG_INF, acc * LOG2_E)


def _rel_bias(table, cq):
    return pl.pallas_call(
        functools.partial(_bias_kernel, cq=cq),
        in_specs=[pl.BlockSpec(memory_space=pltpu.SMEM)],
        out_specs=pl.BlockSpec(memory_space=pltpu.VMEM),
        out_shape=jax.ShapeDtypeStruct((N_HEADS * cq, KEY_PAD), F32),
        name=f"rel_bias_{cq}",
    )(table)


def _head_masks(rows, dtype):
    lane_head = lax.broadcasted_iota(jnp.int32, (rows, GROUP), 1) >> 6
    return [(lane_head == h).astype(F32).astype(dtype) for h in range(HEADS_PER_GROUP)]


def _block_diag_rows(x, masks):
    return jnp.concatenate([x * m for m in masks], axis=0)


def _band_attn_kernel(q_ref, kp_ref, kc_ref, vp_ref, vc_ref, bias_ref, o_ref, kbuf, vbuf, *, tq):
    kbuf[0:tq, :] = kp_ref[0]
    kbuf[tq:2 * tq, :] = kc_ref[0]
    kbuf[2 * tq:, :] = jnp.zeros((CHUNK, ATT_WIDTH), BF16)
    vbuf[0:tq, :] = vp_ref[0]
    vbuf[tq:2 * tq, :] = vc_ref[0]
    vbuf[2 * tq:, :] = jnp.zeros((CHUNK, ATT_WIDTH), BF16)
    masks_b = _head_masks(CHUNK, BF16)
    masks_f = _head_masks(CHUNK, F32)
    rows = HEADS_PER_GROUP * CHUNK
    cells = [(j, g) for j in range(tq // CHUNK) for g in range(N_GROUPS)]
    lanes = [slice(g * GROUP, (g + 1) * GROUP) for g in range(N_GROUPS)]
    win = [slice(j * CHUNK + tq - ATT_WINDOW, j * CHUNK + tq - ATT_WINDOW + KEY_PAD)
           for j in range(tq // CHUNK)]

    def run(first_tile):
        def scores(j, g):
            lhs = _block_diag_rows(q_ref[0, j * CHUNK:(j + 1) * CHUNK, lanes[g]], masks_b)
            return _dot_nt(lhs, kbuf[win[j], lanes[g]])

        def softmax(j, g, s):
            s = s + bias_ref[g * rows:(g + 1) * rows, :]
            if first_tile:
                kj = lax.broadcasted_iota(jnp.int32, (rows, KEY_PAD), 1)
                s = jnp.where(kj >= tq - j * CHUNK, s, NEG_INF)
            p = jnp.exp2(s - jnp.max(s, axis=-1, keepdims=True))
            return p.astype(BF16), 1.0 / jnp.sum(p, axis=-1, keepdims=True)

        def values(j, g, p, inv):
            o_full = _dot(p, vbuf[win[j], lanes[g]]) * inv
            o = o_full[0:CHUNK] * masks_f[0]
            for h in range(1, HEADS_PER_GROUP):
                o = o + o_full[h * CHUNK:(h + 1) * CHUNK] * masks_f[h]
            return o

        s, pz, o = {}, {}, {}
        for n in range(len(cells) + 2):
            if n < len(cells):
                s[n] = scores(*cells[n])
            if 1 <= n <= len(cells):
                pz[n - 1] = softmax(*cells[n - 1], s.pop(n - 1))
            if n >= 2:
                j, g = cells[n - 2]
                o[g] = values(j, g, *pz.pop(n - 2))
                if g == N_GROUPS - 1:
                    o_ref[0, j * CHUNK:(j + 1) * CHUNK, :] = jnp.concatenate(
                        [o[gg] for gg in range(N_GROUPS)], axis=1).astype(BF16)

    @pl.when(pl.program_id(1) == 0)
    def _():
        run(True)

    @pl.when(pl.program_id(1) > 0)
    def _():
        run(False)


def _band_attn(p_att, bias, tq):
    b, t, _ = p_att.shape
    assert tq == ATT_WINDOW and t % tq == 0
    cur = lambda col: (lambda bi, i: (bi, i, col))
    prev = lambda col: (lambda bi, i: (bi, jnp.maximum(i - 1, 0), col))
    blk = (1, tq, ATT_WIDTH)
    return pl.pallas_call(
        functools.partial(_band_attn_kernel, tq=tq),
        grid=(b, t // tq),
        in_specs=[pl.BlockSpec(blk, cur(0)), pl.BlockSpec(blk, prev(1)), pl.BlockSpec(blk, cur(1)),
                  pl.BlockSpec(blk, prev(2)), pl.BlockSpec(blk, cur(2)),
                  _const_spec((N_HEADS * CHUNK, KEY_PAD))],
        out_specs=pl.BlockSpec(blk, cur(0)),
        out_shape=jax.ShapeDtypeStruct((b, t, ATT_WIDTH), BF16),
        scratch_shapes=[pltpu.VMEM((2 * tq + CHUNK, ATT_WIDTH), BF16),
                        pltpu.VMEM((2 * tq + CHUNK, ATT_WIDTH), BF16)],
        compiler_params=pltpu.CompilerParams(dimension_semantics=("arbitrary", "arbitrary"),
                                             vmem_limit_bytes=VMEM_LIMIT_BYTES),
        name="band_attn",
    )(p_att, p_att, p_att, p_att, p_att, bias)


def _step_attn_kernel(q_ref, kc_ref, vc_ref, kn_ref, vn_ref, bias_ref, o_ref, *, tn):
    for h in range(N_HEADS):
        q = q_ref[0, h].astype(BF16)
        b = bias_ref[h * tn:(h + 1) * tn, :]
        s1 = _dot_nt(q, kc_ref[0, h].astype(BF16)) + b[:, :ATT_WINDOW]
        s2 = _dot_nt(q, kn_ref[0, h].astype(BF16)) + b[:, ATT_WINDOW:ATT_WINDOW + tn]
        m = jnp.maximum(jnp.max(s1, axis=-1, keepdims=True), jnp.max(s2, axis=-1, keepdims=True))
        p1 = jnp.exp2(s1 - m)
        p2 = jnp.exp2(s2 - m)
        inv = 1.0 / (jnp.sum(p1, axis=-1, keepdims=True) + jnp.sum(p2, axis=-1, keepdims=True))
        o = _dot(p1.astype(BF16), vc_ref[0, h].astype(BF16)) + _dot(p2.astype(BF16), vn_ref[0, h].astype(BF16))
        o_ref[0, h] = o * inv


def _step_attn(q, k_cache, v_cache, k_new, v_new, bias):
    b, _, tn, _ = q.shape
    new = pl.BlockSpec((1, N_HEADS, tn, HEAD_DIM), lambda i: (i, 0, 0, 0))
    old = pl.BlockSpec((1, N_HEADS, ATT_WINDOW, HEAD_DIM), lambda i: (i, 0, 0, 0))
    return pl.pallas_call(
        functools.partial(_step_attn_kernel, tn=tn),
        grid=(b,),
        in_specs=[new, old, old, new, new, _const_spec((N_HEADS * tn, KEY_PAD))],
        out_specs=new,
        out_shape=jax.ShapeDtypeStruct((b, N_HEADS, tn, HEAD_DIM), F32),
        compiler_params=pltpu.CompilerParams(dimension_semantics=("arbitrary",)),
        name="step_attn",
    )(q, k_cache, v_cache, k_new, v_new, bias)


def _split3(x):
    hi = x.astype(BF16)
    r1 = x - hi.astype(F32)
    mid = r1.astype(BF16)
    lo = (r1 - mid.astype(F32)).astype(BF16)
    return hi, mid, lo


def _wkv_stages(load, state, tri3, hm, bdm, eye4, lanes_of, n_chunks, burst=None, chain_gap=1):
    c = CHUNK
    burst = burst or len(lanes_of) * n_chunks
    t_row = lax.broadcasted_iota(jnp.int32, (c, GROUP), 0)
    s_col = lax.broadcasted_iota(jnp.int32, (c, GROUP), 1) & (c - 1)
    strict = s_col < t_row
    incl = s_col <= t_row
    zero = jnp.zeros((), F32)

    def bd(x):
        return jnp.concatenate([x] * HEADS_PER_GROUP, axis=0) * hm

    cells = [(lane, ci) for ci in range(n_chunks) for lane in lanes_of]

    def each(fn):
        out = {}
        for n, z in enumerate(cells):
            out[z] = fn(z)
            if (n + 1) % burst == 0:
                yield 1
        if len(cells) % burst:
            yield 1
        return out

    lam = {z: load("lam", *z) for z in cells}
    cum = yield from each(lambda z: _dot(tri3, jnp.concatenate(_split3(lam[z]), axis=0)))
    tot = {z: cum[z][c - 1:c, :] for z in cells}
    lhs, bt, kt, bk, v_bd = {}, {}, {}, {}, {}
    for z in cells:
        r, k, kk, beta = (load(name, *z).astype(F32) for name in ("r", "k", "kk", "beta"))
        e_neg = jnp.exp(-cum[z])
        e_rem = jnp.exp(tot[z] - cum[z])
        lhs[z] = jnp.concatenate([-kk * jnp.exp(cum[z] - lam[z]), r * jnp.exp(cum[z])],
                                 axis=0).astype(BF16)
        bt[z] = bd((beta * e_neg).astype(BF16))
        kt[z] = bd((k * e_neg).astype(BF16))
        bk[z] = jnp.concatenate([beta * e_rem, k * e_rem], axis=0).astype(BF16)
        v_bd[z] = bd(load("v", *z))
    ab = yield from each(lambda z: _dot_nt(lhs[z], bt[z]))
    ak = yield from each(lambda z: _dot_nt(lhs[z], kt[z]))
    a_rb = {z: jnp.where(incl, ab[z][c:], zero).astype(BF16) for z in cells}
    a_ak = {z: jnp.where(strict, ak[z][:c], zero).astype(BF16) for z in cells}
    a_rk = {z: jnp.where(incl, ak[z][c:], zero).astype(BF16) for z in cells}
    n_k = {z: jnp.where(strict, ab[z][:c], zero) for z in cells}
    t_m = {z: eye4 + n_k[z] for z in cells}
    n_b = {z: n_k[z].astype(BF16) for z in cells}
    n_b = yield from each(lambda z: _dot(n_b[z], bd(n_b[z])).astype(BF16))
    for _ in range(int(math.log2(c)) - 2):
        both = yield from each(
            lambda z: _dot(jnp.concatenate([t_m[z].astype(BF16), n_b[z]], axis=0), bd(n_b[z])))
        t_m = {z: t_m[z] + both[z][:c] for z in cells}
        n_b = {z: both[z][c:].astype(BF16) for z in cells}
    t_b = yield from each(lambda z: (t_m[z] + _dot(t_m[z].astype(BF16), bd(n_b[z]))).astype(BF16))
    wy_v = yield from each(lambda z: _dot(jnp.concatenate([a_ak[z], a_rk[z]], axis=0), v_bd[z]))
    w_v = {z: wy_v[z][:c] for z in cells}
    y_v = {z: wy_v[z][c:] for z in cells}

    def outputs(zs, x, u):
        return [(z[0], z[1], x[z][c:] + y_v[z] + _dot(a_rb[z], bd(u[z].astype(BF16)))) for z in zs]

    pending = None
    for ci in range(n_chunks):
        zs = [(lane, ci) for lane in lanes_of]
        x = {z: _dot_nt(lhs[z], state[z[0]].astype(BF16)) for z in zs}
        if pending is not None:
            yield from outputs(*pending)
        yield chain_gap
        u = {z: _dot(t_b[z], bd((x[z][:c] + w_v[z]).astype(BF16))) for z in zs}
        yield chain_gap
        for z in zs:
            uv_t = jnp.concatenate([u[z], load("v", *z).astype(F32)], axis=0).T.astype(BF16)
            state[z[0]] = state[z[0]] * jnp.exp(tot[z]) + _dot(uv_t, bk[z]) * bdm
        yield chain_gap
        pending = (zs, x, u)
    yield from outputs(*pending)


def _wkv_kernel(r_ref, k_ref, v_ref, kk_ref, beta_ref, lam_ref, s0_ref, tri3_ref, hm_ref, bdm_ref, eye4_ref,
                y_ref, s_out_ref, s_scr, *, nb, tt):
    ti = pl.program_id(1)

    @pl.when(ti == 0)
    def _():
        s_scr[...] = s0_ref[...]

    c = CHUNK
    n_chunks = -(-tt // c)
    pad = n_chunks * c - tt
    refs = dict(r=r_ref, k=k_ref, v=v_ref, kk=kk_ref, beta=beta_ref, lam=lam_ref)

    def load(name, lane, ci):
        b, g = lane
        lanes = slice(g * GROUP, (g + 1) * GROUP)
        if pad == 0:
            return refs[name][b, ci * c:(ci + 1) * c, lanes]
        x = refs[name][b, :, lanes]
        return jnp.concatenate([x, jnp.zeros((pad, GROUP), x.dtype)], axis=0)

    lanes_of = [(b, g) for b in range(nb) for g in range(N_GROUPS)]
    state = {lane: s_scr[lane[0], lane[1]] for lane in lanes_of}
    y = {}
    for out in _wkv_stages(load, state, tri3_ref[...], hm_ref[...], bdm_ref[...], eye4_ref[...],
                           lanes_of, n_chunks):
        if not isinstance(out, int):
            lane, ci, y_c = out
            y[(lane, ci)] = y_c
    for lane in lanes_of:
        s_scr[lane[0], lane[1]] = state[lane]
    for b in range(nb):
        rows = [jnp.concatenate([y[((b, g), ci)] for g in range(N_GROUPS)], axis=1) for ci in range(n_chunks)]
        y_b = rows[0] if n_chunks == 1 else jnp.concatenate(rows, axis=0)
        y_ref[b] = y_b[:tt].astype(BF16)

    @pl.when(ti == pl.num_programs(1) - 1)
    def _():
        s_out_ref[...] = s_scr[...]


def _alternate(main, side):
    results = []
    for out in side:
        if isinstance(out, int):
            for _ in range(out):
                next(main, _DONE)
        else:
            results.append(out)
    for _ in main:
        pass
    return results


_DONE = object()


def _wkv_out_ffn_kernel(x1_ref, bonus_ref, g_ref, at_ref, r_ref, k_ref, v_ref, kk_ref, beta_ref, lam_ref,
                        lnw_ref, lnb_ref, ones_ref, wor_ref, woa_ref, g2_ref, wg_ref, wu_ref, wd_ref, gf_ref,
                        tri3_ref, bdm_ref, eye4_ref, o_ref, s_out_ref, y_scr, s_scr, *, tm, tiles_per_seq):
    i = pl.program_id(0)
    n_tiles = pl.num_programs(0) - 1
    tile = jnp.minimum(i, n_tiles - 1)

    @pl.when(i == 0)
    def _():
        y_scr[...] = jnp.zeros_like(y_scr)

    @pl.when(tile % tiles_per_seq == 0)
    def _():
        s_scr[...] = jnp.zeros_like(s_scr)

    parts = _row_parts(tm)
    inv_n = 1.0 / HEAD_DIM
    y_prev = [y_scr[rs, :] for rs in parts]
    ff_cols = [slice(j, min(j + FF_CHUNK, D_FF)) for j in range(0, D_FF, FF_CHUNK)]

    def ffn_stages():
        x2, h = [], []
        for i_p, rs in enumerate(parts):
            mean = _head_sums(y_prev[i_p], ones_ref) * inv_n
            yield
            d = y_prev[i_p].astype(F32) - mean
            var = _head_sums((d * d).astype(BF16), ones_ref) * inv_n
            yield
            yn = d * lax.rsqrt(var + GN_EPS) * lnw_ref[...] + lnb_ref[...]
            ro = ((yn + bonus_ref[rs, :].astype(F32)) * g_ref[rs, :].astype(F32)).astype(BF16)
            x2.append(x1_ref[rs, :] + _dot(ro, wor_ref[...]) + _dot(at_ref[rs, :], woa_ref[...]))
            yield
            h.append(_rms(x2[i_p], g2_ref[...]).astype(BF16))
        acc = [None] * len(parts)
        for cols in ff_cols:
            gu = []
            for i_p in range(len(parts)):
                gate = _dot(h[i_p], wg_ref[:, cols])
                yield
                up = _dot(h[i_p], wu_ref[:, cols])
                yield
                gu.append((gate, up))
            for i_p, (gate, up) in enumerate(gu):
                act = (gate * jax.nn.sigmoid(gate) * up).astype(BF16)
                down = _dot(act, wd_ref[cols, :])
                acc[i_p] = down if acc[i_p] is None else acc[i_p] + down
                yield
        for i_p, rs in enumerate(parts):
            o_ref[rs, :] = _rms(x2[i_p] + 0.5 * acc[i_p], gf_ref[...])

    c = CHUNK
    refs = dict(r=r_ref, k=k_ref, v=v_ref, kk=kk_ref, beta=beta_ref, lam=lam_ref)

    def load(name, g, ci):
        return refs[name][ci * c:(ci + 1) * c, g * GROUP:(g + 1) * GROUP]

    lanes_of = list(range(N_GROUPS))
    state = {g: s_scr[g] for g in lanes_of}
    wkv = _wkv_stages(load, state, tri3_ref[...], ones_ref[...], bdm_ref[...], eye4_ref[...], lanes_of, tm // c,
                      burst=FUSED_BURST, chain_gap=FUSED_CHAIN_GAP)
    for g, ci, y_c in _alternate(ffn_stages(), wkv):
        y_scr[ci * c:(ci + 1) * c, g * GROUP:(g + 1) * GROUP] = y_c.astype(BF16)
    for g in lanes_of:
        s_scr[g] = state[g]

    @pl.when(i < n_tiles)
    def _():
        s_out_ref[0] = s_scr[...]


def _wkv_out_ffn(x1, bonus, g, at, r, k, v, kk, beta, lam, lnw, lnb, ones, wor, woa, ffn, gf, t_seq):
    n = x1.shape[0]
    tm = TOKEN_TILE
    assert n % tm == 0 and t_seq % tm == 0
    n_tiles = n // tm
    c = CHUNK
    tri = np.tril(np.ones((c, c), np.float32))
    tri3 = jnp.asarray(np.concatenate([tri, tri, tri], axis=1), BF16)
    bdm = jnp.asarray(_head_block_ones(GROUP))
    eye4 = jnp.asarray(np.tile(np.eye(c, dtype=np.float32), (1, HEADS_PER_GROUP)))
    prev = lambda i: (jnp.maximum(i - 1, 0), 0)
    cur = lambda i: (jnp.minimum(i, n_tiles - 1), 0)
    vec = lambda m: _const_spec((1, m))
    wide_prev = pl.BlockSpec((tm, RWKV_WIDTH), prev)
    wide_cur = pl.BlockSpec((tm, RWKV_WIDTH), cur)
    state = pl.BlockSpec((1, N_GROUPS, GROUP, GROUP),
                         lambda i: (jnp.minimum(i, n_tiles - 1) // (t_seq // tm), 0, 0, 0))
    return pl.pallas_call(
        functools.partial(_wkv_out_ffn_kernel, tm=tm, tiles_per_seq=t_seq // tm),
        grid=(n_tiles + 1,),
        in_specs=[pl.BlockSpec((tm, D_MODEL), prev), wide_prev, wide_prev, wide_prev,
                  wide_cur, wide_cur, wide_cur, wide_cur, wide_cur, wide_cur,
                  vec(RWKV_WIDTH), vec(RWKV_WIDTH), _const_spec((GROUP, GROUP)),
                  _const_spec((RWKV_WIDTH, D_MODEL)), _const_spec((ATT_WIDTH, D_MODEL)),
                  vec(D_MODEL), _const_spec((D_MODEL, D_FF)), _const_spec((D_MODEL, D_FF)),
                  _const_spec((D_FF, D_MODEL)), vec(D_MODEL),
                  _const_spec((c, 3 * c)), _const_spec((GROUP, GROUP)), _const_spec((c, GROUP))],
        out_specs=[pl.BlockSpec((tm, D_MODEL), prev), state],
        out_shape=[jax.ShapeDtypeStruct((n, D_MODEL), F32),
                   jax.ShapeDtypeStruct((n // t_seq, N_GROUPS, GROUP, GROUP), F32)],
        scratch_shapes=[pltpu.VMEM((tm, RWKV_WIDTH), BF16), pltpu.VMEM((N_GROUPS, GROUP, GROUP), F32)],
        compiler_params=pltpu.CompilerParams(dimension_semantics=("arbitrary",),
                                             vmem_limit_bytes=FUSED_VMEM_LIMIT_BYTES),
        name="wkv_out_ffn",
    )(x1, bonus, g, at, r, k, v, kk, beta, lam, lnw, lnb, ones, wor, woa, *ffn, gf, tri3, bdm, eye4)


def _wkv(r, k, v, kk, beta, lam, s0):
    b, t, _ = r.shape
    nb = WKV_BATCH if b % WKV_BATCH == 0 else 1
    tt = min(WKV_TILE, t)
    assert t % tt == 0
    c = CHUNK
    tri = np.tril(np.ones((c, c), np.float32))
    tri3 = jnp.asarray(np.concatenate([tri, tri, tri], axis=1), BF16)
    ones = _head_block_ones(GROUP)
    eye4 = jnp.asarray(np.tile(np.eye(c, dtype=np.float32), (1, HEADS_PER_GROUP)))
    tok = pl.BlockSpec((nb, tt, RWKV_WIDTH), lambda bi, i: (bi, i, 0))
    state = pl.BlockSpec((nb, N_GROUPS, GROUP, GROUP), lambda bi, i: (bi, 0, 0, 0))
    return pl.pallas_call(
        functools.partial(_wkv_kernel, nb=nb, tt=tt),
        grid=(b // nb, t // tt),
        in_specs=[tok, tok, tok, tok, tok, tok, state,
                  _const_spec((c, 3 * c)), _const_spec((GROUP, GROUP)), _const_spec((GROUP, GROUP)),
                  _const_spec((c, GROUP))],
        out_specs=[tok, state],
        out_shape=[jax.ShapeDtypeStruct((b, t, RWKV_WIDTH), BF16),
                   jax.ShapeDtypeStruct((b, N_GROUPS, GROUP, GROUP), F32)],
        scratch_shapes=[pltpu.VMEM((nb, N_GROUPS, GROUP, GROUP), F32)],
        compiler_params=pltpu.CompilerParams(dimension_semantics=("arbitrary", "arbitrary"),
                                             vmem_limit_bytes=VMEM_LIMIT_BYTES),
        name="wkv",
    )(r, k, v, kk, beta, lam, s0, tri3, jnp.asarray(ones, BF16), jnp.asarray(ones), eye4)


def _embed_states(s):
    b = s.shape[0]
    eye = jnp.eye(HEADS_PER_GROUP, dtype=s.dtype)
    s = s.reshape(b, N_GROUPS, HEADS_PER_GROUP, HEAD_DIM, HEAD_DIM)
    bd = s[:, :, :, :, None, :] * eye[None, None, :, None, :, None]
    return bd.reshape(b, N_GROUPS, GROUP, GROUP)


def _extract_states(bd):
    b = bd.shape[0]
    x = bd.reshape(b, N_GROUPS, HEADS_PER_GROUP, HEAD_DIM, HEADS_PER_GROUP, HEAD_DIM)
    blocks = [x[:, :, h, :, h, :] for h in range(HEADS_PER_GROUP)]
    return jnp.stack(blocks, axis=2).reshape(b, N_HEADS, HEAD_DIM, HEAD_DIM)


def _heads_first(x):
    b, t, _ = x.shape
    return x.reshape(b, t, N_HEADS, HEAD_DIM).transpose(0, 2, 1, 3)


def kernel(x_prompt, x_sample, state_shift, state_wkv, cache_attn_k, cache_attn_v, norm_ff1, w_ff1_gate, w_ff1_up, w_ff1_down, norm_mix, w_in, mu_shift, w0, w_lora_up, a0, a_lora_up, g_lora_up, k_k, k_a, r_k, ln_x_w, ln_x_b, rel_bias, w_out, norm_ff2, w_ff2_gate, w_ff2_up, w_ff2_down, norm_final):
    assert norm_ff1.shape[0] == 1, "single layer"
    bp, tp, _ = x_prompt.shape
    bs, ts, _ = x_sample.shape
    assert tp % ATT_WINDOW == 0 and ts <= CHUNK and cache_attn_k.shape[3] == ATT_WINDOW

    row = lambda w: w[0].reshape(1, -1).astype(F32)
    bf = lambda w: w[0].astype(BF16)
    ffn1 = (row(norm_ff1), bf(w_ff1_gate), bf(w_ff1_up), bf(w_ff1_down))
    ffn2 = (row(norm_ff2), bf(w_ff2_gate), bf(w_ff2_up), bf(w_ff2_down))
    w_out_b = bf(w_out)
    norm_final_row = norm_final.reshape(1, D_MODEL).astype(F32)
    lora_pad = jnp.zeros((HEAD_DIM, RWKV_WIDTH), BF16)
    gates = (row(mu_shift), row(w0), jnp.concatenate([bf(w_lora_up), lora_pad], axis=0), row(a0),
             jnp.concatenate([lora_pad, bf(a_lora_up)], axis=0), bf(g_lora_up), row(k_k), row(k_a), row(r_k))
    ones = jnp.asarray(_head_block_ones(GROUP), BF16)

    def front(x, first, t_seq):
        return _ffn_inproj(x.reshape(-1, D_MODEL), first, ffn1, row(norm_mix), bf(w_in), gates, ones, t_seq)

    tail = (row(ln_x_w), row(ln_x_b), ones, w_out_b[:RWKV_WIDTH], w_out_b[RWKV_WIDTH:], ffn2, norm_final_row)

    x1, pa, r, k, v, kk, beta, lam, g, bonus, plast = front(x_prompt, jnp.zeros((bp, 1, RWKV_COLS), F32), tp)
    pa = pa.reshape(bp, tp, ATT_COLS)
    at = _band_attn(pa, _rel_bias(rel_bias[0], CHUNK), ATT_WINDOW)
    y_prompt, s_bd = _wkv_out_ffn(x1, bonus, g, at.reshape(bp * tp, ATT_WIDTH), r, k, v, kk, beta, lam,
                                  *tail, tp)
    y_prompt = y_prompt.reshape(bp, tp, D_MODEL)
    p_shift = plast.reshape(bp, -1, RWKV_COLS)[:, -1, :][None]
    p_wkv = _extract_states(s_bd)[None]
    p_k = _heads_first(pa[:, tp - ATT_WINDOW:, ATT_WIDTH:2 * ATT_WIDTH].astype(F32))[None]
    p_v = _heads_first(pa[:, tp - ATT_WINDOW:, 2 * ATT_WIDTH:].astype(F32))[None]

    x1, pa, r, k, v, kk, beta, lam, g, bonus, plast = front(x_sample, jnp.repeat(state_shift[0], ts, axis=0), ts)
    seq = lambda z: z.reshape(bs, ts, RWKV_WIDTH)
    y, s_bd = _wkv(seq(r), seq(k), seq(v), seq(kk), seq(beta), seq(lam), _embed_states(state_wkv[0]))
    pa = pa.reshape(bs, ts, ATT_COLS)
    q = _heads_first(pa[:, :, :ATT_WIDTH].astype(F32))
    s_k = _heads_first(pa[:, :, ATT_WIDTH:2 * ATT_WIDTH].astype(F32))
    s_v = _heads_first(pa[:, :, 2 * ATT_WIDTH:].astype(F32))
    at = _step_attn(q, cache_attn_k[0], cache_attn_v[0], s_k, s_v, _rel_bias(rel_bias[0], ts))
    at = at.transpose(0, 2, 1, 3).reshape(bs * ts, ATT_WIDTH).astype(BF16)
    y_sample = _out_ffn(x1, y.reshape(bs * ts, RWKV_WIDTH), bonus, g, at, *tail).reshape(bs, ts, D_MODEL)
    s_shift = plast.reshape(bs, ts, RWKV_COLS)[:, -1, :][None]
    s_wkv = _extract_states(s_bd)[None]

    return (y_prompt, y_sample, p_shift, p_wkv, p_k, p_v, s_shift, s_wkv, s_k[None], s_v[None])
```

```python
import functools
import math

import numpy as np
import jax
import jax.numpy as jnp
from jax import lax
from jax.experimental import pallas as pl
from jax.experimental.pallas import tpu as pltpu

F32 = jnp.float32
BF16 = jnp.bfloat16

D_MODEL = 1024
HEAD_DIM = 64
RWKV_WIDTH = 512
ATT_WIDTH = 512
N_HEADS = 8
LORA_COLS = 256
RWKV_COLS = 3 * RWKV_WIDTH + LORA_COLS
ATT_COLS = 3 * ATT_WIDTH
D_FF = 2816
CHUNK = 64
ATT_WINDOW = 512
REL_MAX = 128
NORM_EPS = 1e-5
GN_EPS = 64e-5
NEG_INF = -1e30

GROUP = 256
HEADS_PER_GROUP = GROUP // HEAD_DIM
N_GROUPS = RWKV_WIDTH // GROUP
KEY_PAD = ATT_WINDOW + 2 * CHUNK
LOG2_E = math.log2(math.e)
QK_SCALE = HEAD_DIM ** -0.5 * LOG2_E

TOKEN_TILE = 512
ROW_PART = 256
WKV_BATCH = 4
WKV_TILE = 128
FF_CHUNK = 256
FUSED_BURST = 8
FUSED_CHAIN_GAP = 2
VMEM_LIMIT_BYTES = 56 * 1024 * 1024
FUSED_VMEM_LIMIT_BYTES = 60 * 1024 * 1024


def _rms(x, g):
    ms = jnp.mean(x * x, axis=-1, keepdims=True)
    return x * lax.rsqrt(ms + NORM_EPS) * g


def _dot(a, b):
    return jnp.dot(a, b, preferred_element_type=F32)


def _dot_nt(a, b):
    return lax.dot_general(a, b, (((1,), (1,)), ((), ())), preferred_element_type=F32)


def _row_parts(tm):
    n = max(1, tm // ROW_PART)
    return [slice(i * (tm // n), (i + 1) * (tm // n)) for i in range(n)]


def _head_sums(x, ones_ref):
    return jnp.concatenate([_dot(x[:, g * GROUP:(g + 1) * GROUP], ones_ref[...]) for g in range(N_GROUPS)],
                           axis=1)


def _const_spec(shape):
    return pl.BlockSpec(shape, lambda *_: (0,) * len(shape), pipeline_mode=pl.Buffered(1))


def _head_block_ones(n):
    h = np.arange(n) // HEAD_DIM
    return (h[:, None] == h[None, :]).astype(np.float32)


def _ffn_inproj_kernel(x_ref, first_ref, g1_ref, wg_ref, wu_ref, wd_ref, gm_ref, win_ref,
                       mu_ref, w0_ref, wl_ref, a0_ref, al_ref, gl_ref, kkw_ref, ka_ref, rk_ref, ones_ref,
                       x1_ref, pa_ref, r_ref, k_ref, v_ref, kk_ref, beta_ref, lam_ref, g_ref, bonus_ref,
                       plast_ref, carry_scr, *, tm, t_seq):
    parts = _row_parts(tm)
    rows = parts[0].stop
    q_cols = slice(RWKV_COLS, RWKV_COLS + ATT_WIDTH)

    def gate_up(rs):
        h = _rms(x_ref[rs, :], g1_ref[...]).astype(BF16)
        return _dot(h, wg_ref[...]), _dot(h, wu_ref[...])

    def down_inproj(rs, gate, up):
        act = (gate * jax.nn.sigmoid(gate) * up).astype(BF16)
        x1 = x_ref[rs, :] + 0.5 * _dot(act, wd_ref[...])
        x1_ref[rs, :] = x1
        h2 = _rms(x1, gm_ref[...]).astype(BF16)
        p = _dot(h2, win_ref[:, :RWKV_COLS])
        pa_ref[rs, :ATT_WIDTH] = (_dot(h2, win_ref[:, q_cols]) * QK_SCALE).astype(BF16)
        pa_ref[rs, ATT_WIDTH:] = _dot(h2, win_ref[:, RWKV_COLS + ATT_WIDTH:]).astype(BF16)
        return p

    def gates(rs, p, before):
        row = lax.broadcasted_iota(jnp.int32, (rows, RWKV_COLS), 0)
        first_row = (row == 0) if t_seq >= tm else ((row & (t_seq - 1)) == 0)
        xs = p + (jnp.where(first_row, before, pltpu.roll(p, 1, axis=0)) - p) * mu_ref[...]
        r = xs[:, 0:RWKV_WIDTH]
        k = xs[:, RWKV_WIDTH:2 * RWKV_WIDTH]
        v = xs[:, 2 * RWKV_WIDTH:3 * RWKV_WIDTH]
        lora_in = xs[:, 3 * RWKV_WIDTH:3 * RWKV_WIDTH + 128]
        gate_in = xs[:, 3 * RWKV_WIDTH + 128:]
        z = w0_ref[...] + _dot(jnp.tanh(lora_in).astype(BF16), wl_ref[...])
        a = jax.nn.sigmoid(a0_ref[...] + _dot(lora_in.astype(BF16), al_ref[...]))
        g_ref[rs, :] = _dot(jax.nn.sigmoid(gate_in).astype(BF16), gl_ref[...]).astype(BF16)
        kk = k * kkw_ref[...]
        kk = kk * lax.rsqrt(jnp.maximum(_head_sums((kk * kk).astype(BF16), ones_ref), 1e-24))
        k = k * (1.0 + (a - 1.0) * ka_ref[...])
        lam_ref[rs, :] = -math.exp(-0.5) * jax.nn.sigmoid(z)
        r_ref[rs, :] = r.astype(BF16)
        k_ref[rs, :] = k.astype(BF16)
        v_ref[rs, :] = v.astype(BF16)
        kk_ref[rs, :] = kk.astype(BF16)
        beta_ref[rs, :] = (a * kk).astype(BF16)
        bonus_ref[rs, :] = (_head_sums((r * k * rk_ref[...]).astype(BF16), ones_ref) * v).astype(BF16)

    if t_seq >= tm:
        @pl.when(pl.program_id(0) % (t_seq // tm) == 0)
        def _():
            carry_scr[...] = first_ref[0]
        before = carry_scr[...]
    else:
        assert len(parts) == 1
        before = first_ref[...]

    gu = gate_up(parts[0])
    for i, rs in enumerate(parts):
        p = down_inproj(rs, *gu)
        if i + 1 < len(parts):
            gu = gate_up(parts[i + 1])
        gates(rs, p, before)
        before = p[rows - 1:rows, :]
    if t_seq >= tm:
        carry_scr[...] = before
        plast_ref[0] = before
    else:
        plast_ref[...] = p


def _ffn_inproj(x, first, ffn, gm, win, gates, ones, t_seq):
    n = x.shape[0]
    tm = min(TOKEN_TILE, n)
    assert n % tm == 0 and (t_seq % tm == 0 or tm % t_seq == 0) and t_seq & (t_seq - 1) == 0
    row = lambda i: (i, 0)
    vec = lambda m: _const_spec((1, m))
    wide = lambda dt: jax.ShapeDtypeStruct((n, RWKV_WIDTH), dt)
    wide_spec = pl.BlockSpec((tm, RWKV_WIDTH), row)
    if t_seq >= tm:
        first_spec = pl.BlockSpec((1, 1, RWKV_COLS), lambda i: (i // (t_seq // tm), 0, 0))
        plast_shape = jax.ShapeDtypeStruct((n // tm, 1, RWKV_COLS), F32)
        plast_spec = pl.BlockSpec((1, 1, RWKV_COLS), lambda i: (i, 0, 0))
    else:
        first_spec = pl.BlockSpec((tm, RWKV_COLS), row)
        plast_shape = jax.ShapeDtypeStruct((n, RWKV_COLS), F32)
        plast_spec = pl.BlockSpec((tm, RWKV_COLS), row)
    return pl.pallas_call(
        functools.partial(_ffn_inproj_kernel, tm=tm, t_seq=t_seq),
        grid=(n // tm,),
        in_specs=[pl.BlockSpec((tm, D_MODEL), row), first_spec,
                  vec(D_MODEL), _const_spec((D_MODEL, D_FF)), _const_spec((D_MODEL, D_FF)),
                  _const_spec((D_FF, D_MODEL)), vec(D_MODEL), _const_spec((D_MODEL, RWKV_COLS + ATT_COLS)),
                  vec(RWKV_COLS), vec(RWKV_WIDTH), _const_spec((128, RWKV_WIDTH)), vec(RWKV_WIDTH),
                  _const_spec((128, RWKV_WIDTH)), _const_spec((128, RWKV_WIDTH)),
                  vec(RWKV_WIDTH), vec(RWKV_WIDTH), vec(RWKV_WIDTH), _const_spec((GROUP, GROUP))],
        out_specs=[pl.BlockSpec((tm, D_MODEL), row), pl.BlockSpec((tm, ATT_COLS), row)]
                  + [wide_spec] * 8 + [plast_spec],
        out_shape=[jax.ShapeDtypeStruct((n, D_MODEL), F32), jax.ShapeDtypeStruct((n, ATT_COLS), BF16),
                   wide(BF16), wide(BF16), wide(BF16), wide(BF16), wide(BF16), wide(F32), wide(BF16),
                   wide(BF16), plast_shape],
        scratch_shapes=[pltpu.VMEM((1, RWKV_COLS), F32)],
        compiler_params=pltpu.CompilerParams(dimension_semantics=("arbitrary",),
                                             vmem_limit_bytes=VMEM_LIMIT_BYTES),
        name="ffn_inproj",
    )(x, first, *ffn, gm, win, *gates, ones)


def _out_ffn_kernel(x1_ref, y_ref, bonus_ref, g_ref, at_ref, lnw_ref, lnb_ref, ones_ref, wor_ref, woa_ref,
                    g2_ref, wg_ref, wu_ref, wd_ref, gf_ref, o_ref):
    parts = _row_parts(x1_ref.shape[0])
    nparts = range(len(parts))
    inv_n = 1.0 / HEAD_DIM
    mean = [_head_sums(y_ref[rs, :], ones_ref) * inv_n for rs in parts]
    d = [y_ref[parts[i], :].astype(F32) - mean[i] for i in nparts]
    var = [_head_sums((t * t).astype(BF16), ones_ref) * inv_n for t in d]
    ro = []
    for i in nparts:
        rs = parts[i]
        yn = d[i] * lax.rsqrt(var[i] + GN_EPS) * lnw_ref[...] + lnb_ref[...]
        ro.append(((yn + bonus_ref[rs, :].astype(F32)) * g_ref[rs, :].astype(F32)).astype(BF16))
    x2 = [x1_ref[parts[i], :] + _dot(ro[i], wor_ref[...]) + _dot(at_ref[parts[i], :], woa_ref[...])
          for i in nparts]
    h = [_rms(t, g2_ref[...]).astype(BF16) for t in x2]
    gu = [(_dot(hi, wg_ref[...]), _dot(hi, wu_ref[...])) for hi in h]
    act = [(gate * jax.nn.sigmoid(gate) * up).astype(BF16) for gate, up in gu]
    x3 = [x2[i] + 0.5 * _dot(act[i], wd_ref[...]) for i in nparts]
    for i in nparts:
        o_ref[parts[i], :] = _rms(x3[i], gf_ref[...])


def _out_ffn(x1, y, bonus, g, at, lnw, lnb, ones, wor, woa, ffn, gf):
    n = x1.shape[0]
    tm = min(TOKEN_TILE, n)
    row = lambda i: (i, 0)
    vec = lambda m: _const_spec((1, m))
    wide_spec = pl.BlockSpec((tm, RWKV_WIDTH), row)
    return pl.pallas_call(
        _out_ffn_kernel,
        grid=(n // tm,),
        in_specs=[pl.BlockSpec((tm, D_MODEL), row), wide_spec, wide_spec, wide_spec, wide_spec,
                  vec(RWKV_WIDTH), vec(RWKV_WIDTH), _const_spec((GROUP, GROUP)),
                  _const_spec((RWKV_WIDTH, D_MODEL)), _const_spec((ATT_WIDTH, D_MODEL)),
                  vec(D_MODEL), _const_spec((D_MODEL, D_FF)), _const_spec((D_MODEL, D_FF)),
                  _const_spec((D_FF, D_MODEL)), vec(D_MODEL)],
        out_specs=pl.BlockSpec((tm, D_MODEL), row),
        out_shape=jax.ShapeDtypeStruct((n, D_MODEL), F32),
        compiler_params=pltpu.CompilerParams(dimension_semantics=("arbitrary",),
                                             vmem_limit_bytes=VMEM_LIMIT_BYTES),
        name="out_ffn",
    )(x1, y, bonus, g, at, lnw, lnb, ones, wor, woa, *ffn, gf)


def _bias_kernel(tab_ref, o_ref, *, cq):
    qi = lax.broadcasted_iota(jnp.int32, (cq, KEY_PAD), 0)
    kj = lax.broadcasted_iota(jnp.int32, (cq, KEY_PAD), 1)
    idx = jnp.clip(kj - ATT_WINDOW - qi, -REL_MAX, REL_MAX) + REL_MAX
    pad = kj >= ATT_WINDOW + cq
    for h in range(N_HEADS):
        def body(t, acc):
            return jnp.where(idx == t, tab_ref[h, t], acc)
        acc = lax.fori_loop(0, REL_MAX + cq, body, jnp.zeros((cq, KEY_PAD), F32))
        o_ref[h * cq:(h + 1) * cq, :] = jnp.where(pad, NEG_INF, acc * LOG2_E)


def _rel_bias(table, cq):
    return pl.pallas_call(
        functools.partial(_bias_kernel, cq=cq),
        in_specs=[pl.BlockSpec(memory_space=pltpu.SMEM)],
        out_specs=pl.BlockSpec(memory_space=pltpu.VMEM),
        out_shape=jax.ShapeDtypeStruct((N_HEADS * cq, KEY_PAD), F32),
        name=f"rel_bias_{cq}",
    )(table)


def _head_masks(rows, dtype):
    lane_head = lax.broadcasted_iota(jnp.int32, (rows, GROUP), 1) >> 6
    return [(lane_head == h).astype(F32).astype(dtype) for h in range(HEADS_PER_GROUP)]


def _block_diag_rows(x, masks):
    return jnp.concatenate([x * m for m in masks], axis=0)


def _band_attn_kernel(q_ref, kp_ref, kc_ref, vp_ref, vc_ref, bias_ref, o_ref, kbuf, vbuf, *, tq):
    kbuf[0:tq, :] = kp_ref[0]
    kbuf[tq:2 * tq, :] = kc_ref[0]
    kbuf[2 * tq:, :] = jnp.zeros((CHUNK, ATT_WIDTH), BF16)
    vbuf[0:tq, :] = vp_ref[0]
    vbuf[tq:2 * tq, :] = vc_ref[0]
    vbuf[2 * tq:, :] = jnp.zeros((CHUNK, ATT_WIDTH), BF16)
    masks_b = _head_masks(CHUNK, BF16)
    masks_f = _head_masks(CHUNK, F32)
    rows = HEADS_PER_GROUP * CHUNK
    cells = [(j, g) for j in range(tq // CHUNK) for g in range(N_GROUPS)]
    lanes = [slice(g * GROUP, (g + 1) * GROUP) for g in range(N_GROUPS)]
    win = [slice(j * CHUNK + tq - ATT_WINDOW, j * CHUNK + tq - ATT_WINDOW + KEY_PAD)
           for j in range(tq // CHUNK)]

    def run(first_tile):
        def scores(j, g):
            lhs = _block_diag_rows(q_ref[0, j * CHUNK:(j + 1) * CHUNK, lanes[g]], masks_b)
            return _dot_nt(lhs, kbuf[win[j], lanes[g]])

        def softmax(j, g, s):
            s = s + bias_ref[g * rows:(g + 1) * rows, :]
            if first_tile:
                kj = lax.broadcasted_iota(jnp.int32, (rows, KEY_PAD), 1)
                s = jnp.where(kj >= tq - j * CHUNK, s, NEG_INF)
            p = jnp.exp2(s - jnp.max(s, axis=-1, keepdims=True))
            return p.astype(BF16), 1.0 / jnp.sum(p, axis=-1, keepdims=True)

        def values(j, g, p, inv):
            o_full = _dot(p, vbuf[win[j], lanes[g]]) * inv
            o = o_full[0:CHUNK] * masks_f[0]
            for h in range(1, HEADS_PER_GROUP):
                o = o + o_full[h * CHUNK:(h + 1) * CHUNK] * masks_f[h]
            return o

        s, pz, o = {}, {}, {}
        for n in range(len(cells) + 2):
            if n < len(cells):
                s[n] = scores(*cells[n])
            if 1 <= n <= len(cells):
                pz[n - 1] = softmax(*cells[n - 1], s.pop(n - 1))
            if n >= 2:
                j, g = cells[n - 2]
                o[g] = values(j, g, *pz.pop(n - 2))
                if g == N_GROUPS - 1:
                    o_ref[0, j * CHUNK:(j + 1) * CHUNK, :] = jnp.concatenate(
                        [o[gg] for gg in range(N_GROUPS)], axis=1).astype(BF16)

    @pl.when(pl.program_id(1) == 0)
    def _():
        run(True)

    @pl.when(pl.program_id(1) > 0)
    def _():
        run(False)


def _band_attn(p_att, bias, tq):
    b, t, _ = p_att.shape
    assert tq == ATT_WINDOW and t % tq == 0
    cur = lambda col: (lambda bi, i: (bi, i, col))
    prev = lambda col: (lambda bi, i: (bi, jnp.maximum(i - 1, 0), col))
    blk = (1, tq, ATT_WIDTH)
    return pl.pallas_call(
        functools.partial(_band_attn_kernel, tq=tq),
        grid=(b, t // tq),
        in_specs=[pl.BlockSpec(blk, cur(0)), pl.BlockSpec(blk, prev(1)), pl.BlockSpec(blk, cur(1)),
                  pl.BlockSpec(blk, prev(2)), pl.BlockSpec(blk, cur(2)),
                  _const_spec((N_HEADS * CHUNK, KEY_PAD))],
        out_specs=pl.BlockSpec(blk, cur(0)),
        out_shape=jax.ShapeDtypeStruct((b, t, ATT_WIDTH), BF16),
        scratch_shapes=[pltpu.VMEM((2 * tq + CHUNK, ATT_WIDTH), BF16),
                        pltpu.VMEM((2 * tq + CHUNK, ATT_WIDTH), BF16)],
        compiler_params=pltpu.CompilerParams(dimension_semantics=("arbitrary", "arbitrary"),
                                             vmem_limit_bytes=VMEM_LIMIT_BYTES),
        name="band_attn",
    )(p_att, p_att, p_att, p_att, p_att, bias)


def _step_attn_kernel(q_ref, kc_ref, vc_ref, kn_ref, vn_ref, bias_ref, o_ref, *, tn):
    heads = range(N_HEADS)
    q = [q_ref[0, h].astype(BF16) for h in heads]
    s1 = [_dot_nt(q[h], kc_ref[0, h].astype(BF16)) for h in heads]
    s2 = [_dot_nt(q[h], kn_ref[0, h].astype(BF16)) for h in heads]
    p1, p2, inv = [], [], []
    for h in heads:
        b = bias_ref[h * tn:(h + 1) * tn, :]
        a1 = s1[h] + b[:, :ATT_WINDOW]
        a2 = s2[h] + b[:, ATT_WINDOW:ATT_WINDOW + tn]
        m = jnp.maximum(jnp.max(a1, axis=-1, keepdims=True), jnp.max(a2, axis=-1, keepdims=True))
        e1 = jnp.exp2(a1 - m)
        e2 = jnp.exp2(a2 - m)
        inv.append(1.0 / (jnp.sum(e1, axis=-1, keepdims=True) + jnp.sum(e2, axis=-1, keepdims=True)))
        p1.append(e1.astype(BF16))
        p2.append(e2.astype(BF16))
    o1 = [_dot(p1[h], vc_ref[0, h].astype(BF16)) for h in heads]
    o2 = [_dot(p2[h], vn_ref[0, h].astype(BF16)) for h in heads]
    for h in heads:
        o_ref[0, h] = (o1[h] + o2[h]) * inv[h]


def _step_attn(q, k_cache, v_cache, k_new, v_new, bias):
    b, _, tn, _ = q.shape
    new = pl.BlockSpec((1, N_HEADS, tn, HEAD_DIM), lambda i: (i, 0, 0, 0))
    old = pl.BlockSpec((1, N_HEADS, ATT_WINDOW, HEAD_DIM), lambda i: (i, 0, 0, 0))
    return pl.pallas_call(
        functools.partial(_step_attn_kernel, tn=tn),
        grid=(b,),
        in_specs=[new, old, old, new, new, _const_spec((N_HEADS * tn, KEY_PAD))],
        out_specs=new,
        out_shape=jax.ShapeDtypeStruct((b, N_HEADS, tn, HEAD_DIM), F32),
        compiler_params=pltpu.CompilerParams(dimension_semantics=("arbitrary",)),
        name="step_attn",
    )(q, k_cache, v_cache, k_new, v_new, bias)


def _split3(x):
    hi = x.astype(BF16)
    r1 = x - hi.astype(F32)
    mid = r1.astype(BF16)
    lo = (r1 - mid.astype(F32)).astype(BF16)
    return hi, mid, lo


def _wkv_stages(load, state, tri3, hm, bdm, eye4, lanes_of, n_chunks, burst=None, chain_gap=1):
    c = CHUNK
    burst = burst or len(lanes_of) * n_chunks
    t_row = lax.broadcasted_iota(jnp.int32, (c, GROUP), 0)
    s_col = lax.broadcasted_iota(jnp.int32, (c, GROUP), 1) & (c - 1)
    strict = s_col < t_row
    incl = s_col <= t_row
    zero = jnp.zeros((), F32)

    def bd(x):
        return jnp.concatenate([x] * HEADS_PER_GROUP, axis=0) * hm

    cells = [(lane, ci) for ci in range(n_chunks) for lane in lanes_of]

    def each(fn):
        out = {}
        for n, z in enumerate(cells):
            out[z] = fn(z)
            if (n + 1) % burst == 0:
                yield 1
        if len(cells) % burst:
            yield 1
        return out

    lam = {z: load("lam", *z) for z in cells}
    cum = yield from each(lambda z: _dot(tri3, jnp.concatenate(_split3(lam[z]), axis=0)))
    tot = {z: cum[z][c - 1:c, :] for z in cells}
    lhs, bt, kt, bk, v_bd = {}, {}, {}, {}, {}
    for z in cells:
        r, k, kk, beta = (load(name, *z).astype(F32) for name in ("r", "k", "kk", "beta"))
        e_neg = jnp.exp(-cum[z])
        e_rem = jnp.exp(tot[z] - cum[z])
        lhs[z] = jnp.concatenate([-kk * jnp.exp(cum[z] - lam[z]), r * jnp.exp(cum[z])],
                                 axis=0).astype(BF16)
        bt[z] = bd((beta * e_neg).astype(BF16))
        kt[z] = bd((k * e_neg).astype(BF16))
        bk[z] = jnp.concatenate([beta * e_rem, k * e_rem], axis=0).astype(BF16)
        v_bd[z] = bd(load("v", *z))
    ab = yield from each(lambda z: _dot_nt(lhs[z], bt[z]))
    ak = yield from each(lambda z: _dot_nt(lhs[z], kt[z]))
    a_rb = {z: jnp.where(incl, ab[z][c:], zero).astype(BF16) for z in cells}
    a_ak = {z: jnp.where(strict, ak[z][:c], zero).astype(BF16) for z in cells}
    a_rk = {z: jnp.where(incl, ak[z][c:], zero).astype(BF16) for z in cells}
    n_k = {z: jnp.where(strict, ab[z][:c], zero) for z in cells}
    t_m = {z: eye4 + n_k[z] for z in cells}
    n_b = {z: n_k[z].astype(BF16) for z in cells}
    n_b = yield from each(lambda z: _dot(n_b[z], bd(n_b[z])).astype(BF16))
    for _ in range(int(math.log2(c)) - 2):
        both = yield from each(
            lambda z: _dot(jnp.concatenate([t_m[z].astype(BF16), n_b[z]], axis=0), bd(n_b[z])))
        t_m = {z: t_m[z] + both[z][:c] for z in cells}
        n_b = {z: both[z][c:].astype(BF16) for z in cells}
    t_b = yield from each(lambda z: (t_m[z] + _dot(t_m[z].astype(BF16), bd(n_b[z]))).astype(BF16))
    wy_v = yield from each(lambda z: _dot(jnp.concatenate([a_ak[z], a_rk[z]], axis=0), v_bd[z]))
    w_v = {z: wy_v[z][:c] for z in cells}
    y_v = {z: wy_v[z][c:] for z in cells}

    def outputs(zs, x, u):
        return [(z[0], z[1], x[z][c:] + y_v[z] + _dot(a_rb[z], bd(u[z].astype(BF16)))) for z in zs]

    pending = None
    for ci in range(n_chunks):
        zs = [(lane, ci) for lane in lanes_of]
        x = {z: _dot_nt(lhs[z], state[z[0]].astype(BF16)) for z in zs}
        if pending is not None:
            yield from outputs(*pending)
        yield chain_gap
        u = {z: _dot(t_b[z], bd((x[z][:c] + w_v[z]).astype(BF16))) for z in zs}
        yield chain_gap
        for z in zs:
            uv_t = jnp.concatenate([u[z], load("v", *z).astype(F32)], axis=0).T.astype(BF16)
            state[z[0]] = state[z[0]] * jnp.exp(tot[z]) + _dot(uv_t, bk[z]) * bdm
        yield chain_gap
        pending = (zs, x, u)
    yield from outputs(*pending)


def _wkv_kernel(r_ref, k_ref, v_ref, kk_ref, beta_ref, lam_ref, s0_ref, tri3_ref, hm_ref, bdm_ref, eye4_ref,
                y_ref, s_out_ref, s_scr, *, nb, tt):
    ti = pl.program_id(1)

    @pl.when(ti == 0)
    def _():
        s_scr[...] = s0_ref[...]

    c = CHUNK
    n_chunks = -(-tt // c)
    pad = n_chunks * c - tt
    refs = dict(r=r_ref, k=k_ref, v=v_ref, kk=kk_ref, beta=beta_ref, lam=lam_ref)

    def load(name, lane, ci):
        b, g = lane
        lanes = slice(g * GROUP, (g + 1) * GROUP)
        if pad == 0:
            return refs[name][b, ci * c:(ci + 1) * c, lanes]
        x = refs[name][b, :, lanes]
        return jnp.concatenate([x, jnp.zeros((pad, GROUP), x.dtype)], axis=0)

    lanes_of = [(b, g) for b in range(nb) for g in range(N_GROUPS)]
    state = {lane: s_scr[lane[0], lane[1]] for lane in lanes_of}
    y = {}
    for out in _wkv_stages(load, state, tri3_ref[...], hm_ref[...], bdm_ref[...], eye4_ref[...],
                           lanes_of, n_chunks):
        if not isinstance(out, int):
            lane, ci, y_c = out
            y[(lane, ci)] = y_c
    for lane in lanes_of:
        s_scr[lane[0], lane[1]] = state[lane]
    for b in range(nb):
        rows = [jnp.concatenate([y[((b, g), ci)] for g in range(N_GROUPS)], axis=1) for ci in range(n_chunks)]
        y_b = rows[0] if n_chunks == 1 else jnp.concatenate(rows, axis=0)
        y_ref[b] = y_b[:tt].astype(BF16)

    @pl.when(ti == pl.num_programs(1) - 1)
    def _():
        s_out_ref[...] = s_scr[...]


def _alternate(main, side):
    results = []
    for out in side:
        if isinstance(out, int):
            for _ in range(out):
                next(main, _DONE)
        else:
            results.append(out)
    for _ in main:
        pass
    return results


_DONE = object()


def _wkv_out_ffn_kernel(x1_ref, bonus_ref, g_ref, at_ref, r_ref, k_ref, v_ref, kk_ref, beta_ref, lam_ref,
                        lnw_ref, lnb_ref, ones_ref, wor_ref, woa_ref, g2_ref, wg_ref, wu_ref, wd_ref, gf_ref,
                        tri3_ref, bdm_ref, eye4_ref, o_ref, s_out_ref, y_scr, s_scr, *, tm, tiles_per_seq):
    i = pl.program_id(0)
    n_tiles = pl.num_programs(0) - 1
    tile = jnp.minimum(i, n_tiles - 1)

    @pl.when(i == 0)
    def _():
        y_scr[...] = jnp.zeros_like(y_scr)

    @pl.when(tile % tiles_per_seq == 0)
    def _():
        s_scr[...] = jnp.zeros_like(s_scr)

    parts = _row_parts(tm)
    inv_n = 1.0 / HEAD_DIM
    y_prev = [y_scr[rs, :] for rs in parts]
    ff_cols = [slice(j, min(j + FF_CHUNK, D_FF)) for j in range(0, D_FF, FF_CHUNK)]

    def ffn_stages():
        x2, h = [], []
        for i_p, rs in enumerate(parts):
            mean = _head_sums(y_prev[i_p], ones_ref) * inv_n
            yield
            d = y_prev[i_p].astype(F32) - mean
            var = _head_sums((d * d).astype(BF16), ones_ref) * inv_n
            yield
            yn = d * lax.rsqrt(var + GN_EPS) * lnw_ref[...] + lnb_ref[...]
            ro = ((yn + bonus_ref[rs, :].astype(F32)) * g_ref[rs, :].astype(F32)).astype(BF16)
            x2.append(x1_ref[rs, :] + _dot(ro, wor_ref[...]) + _dot(at_ref[rs, :], woa_ref[...]))
            yield
            h.append(_rms(x2[i_p], g2_ref[...]).astype(BF16))
        acc = [None] * len(parts)
        for cols in ff_cols:
            gu = []
            for i_p in range(len(parts)):
                gate = _dot(h[i_p], wg_ref[:, cols])
                yield
                up = _dot(h[i_p], wu_ref[:, cols])
                yield
                gu.append((gate, up))
            for i_p, (gate, up) in enumerate(gu):
                act = (gate * jax.nn.sigmoid(gate) * up).astype(BF16)
                down = _dot(act, wd_ref[cols, :])
                acc[i_p] = down if acc[i_p] is None else acc[i_p] + down
                yield
        for i_p, rs in enumerate(parts):
            o_ref[rs, :] = _rms(x2[i_p] + 0.5 * acc[i_p], gf_ref[...])

    c = CHUNK
    refs = dict(r=r_ref, k=k_ref, v=v_ref, kk=kk_ref, beta=beta_ref, lam=lam_ref)

    def load(name, g, ci):
        return refs[name][ci * c:(ci + 1) * c, g * GROUP:(g + 1) * GROUP]

    lanes_of = list(range(N_GROUPS))
    state = {g: s_scr[g] for g in lanes_of}
    wkv = _wkv_stages(load, state, tri3_ref[...], ones_ref[...], bdm_ref[...], eye4_ref[...], lanes_of, tm // c,
                      burst=FUSED_BURST, chain_gap=FUSED_CHAIN_GAP)
    for g, ci, y_c in _alternate(ffn_stages(), wkv):
        y_scr[ci * c:(ci + 1) * c, g * GROUP:(g + 1) * GROUP] = y_c.astype(BF16)
    for g in lanes_of:
        s_scr[g] = state[g]

    @pl.when(i < n_tiles)
    def _():
        s_out_ref[0] = s_scr[...]


def _wkv_out_ffn(x1, bonus, g, at, r, k, v, kk, beta, lam, lnw, lnb, ones, wor, woa, ffn, gf, t_seq):
    n = x1.shape[0]
    tm = TOKEN_TILE
    assert n % tm == 0 and t_seq % tm == 0
    n_tiles = n // tm
    c = CHUNK
    tri = np.tril(np.ones((c, c), np.float32))
    tri3 = jnp.asarray(np.concatenate([tri, tri, tri], axis=1), BF16)
    bdm = jnp.asarray(_head_block_ones(GROUP))
    eye4 = jnp.asarray(np.tile(np.eye(c, dtype=np.float32), (1, HEADS_PER_GROUP)))
    prev = lambda i: (jnp.maximum(i - 1, 0), 0)
    cur = lambda i: (jnp.minimum(i, n_tiles - 1), 0)
    vec = lambda m: _const_spec((1, m))
    wide_prev = pl.BlockSpec((tm, RWKV_WIDTH), prev)
    wide_cur = pl.BlockSpec((tm, RWKV_WIDTH), cur)
    state = pl.BlockSpec((1, N_GROUPS, GROUP, GROUP),
                         lambda i: (jnp.minimum(i, n_tiles - 1) // (t_seq // tm), 0, 0, 0))
    return pl.pallas_call(
        functools.partial(_wkv_out_ffn_kernel, tm=tm, tiles_per_seq=t_seq // tm),
        grid=(n_tiles + 1,),
        in_specs=[pl.BlockSpec((tm, D_MODEL), prev), wide_prev, wide_prev, wide_prev,
                  wide_cur, wide_cur, wide_cur, wide_cur, wide_cur, wide_cur,
                  vec(RWKV_WIDTH), vec(RWKV_WIDTH), _const_spec((GROUP, GROUP)),
                  _const_spec((RWKV_WIDTH, D_MODEL)), _const_spec((ATT_WIDTH, D_MODEL)),
                  vec(D_MODEL), _const_spec((D_MODEL, D_FF)), _const_spec((D_MODEL, D_FF)),
                  _const_spec((D_FF, D_MODEL)), vec(D_MODEL),
                  _const_spec((c, 3 * c)), _const_spec((GROUP, GROUP)), _const_spec((c, GROUP))],
        out_specs=[pl.BlockSpec((tm, D_MODEL), prev), state],
        out_shape=[jax.ShapeDtypeStruct((n, D_MODEL), F32),
                   jax.ShapeDtypeStruct((n // t_seq, N_GROUPS, GROUP, GROUP), F32)],
        scratch_shapes=[pltpu.VMEM((tm, RWKV_WIDTH), BF16), pltpu.VMEM((N_GROUPS, GROUP, GROUP), F32)],
        compiler_params=pltpu.CompilerParams(dimension_semantics=("arbitrary",),
                                             vmem_limit_bytes=FUSED_VMEM_LIMIT_BYTES),
        name="wkv_out_ffn",
    )(x1, bonus, g, at, r, k, v, kk, beta, lam, lnw, lnb, ones, wor, woa, *ffn, gf, tri3, bdm, eye4)


def _wkv(r, k, v, kk, beta, lam, s0):
    b, t, _ = r.shape
    nb = WKV_BATCH if b % WKV_BATCH == 0 else 1
    tt = min(WKV_TILE, t)
    assert t % tt == 0
    c = CHUNK
    tri = np.tril(np.ones((c, c), np.float32))
    tri3 = jnp.asarray(np.concatenate([tri, tri, tri], axis=1), BF16)
    ones = _head_block_ones(GROUP)
    eye4 = jnp.asarray(np.tile(np.eye(c, dtype=np.float32), (1, HEADS_PER_GROUP)))
    tok = pl.BlockSpec((nb, tt, RWKV_WIDTH), lambda bi, i: (bi, i, 0))
    state = pl.BlockSpec((nb, N_GROUPS, GROUP, GROUP), lambda bi, i: (bi, 0, 0, 0))
    return pl.pallas_call(
        functools.partial(_wkv_kernel, nb=nb, tt=tt),
        grid=(b // nb, t // tt),
        in_specs=[tok, tok, tok, tok, tok, tok, state,
                  _const_spec((c, 3 * c)), _const_spec((GROUP, GROUP)), _const_spec((GROUP, GROUP)),
                  _const_spec((c, GROUP))],
        out_specs=[tok, state],
        out_shape=[jax.ShapeDtypeStruct((b, t, RWKV_WIDTH), BF16),
                   jax.ShapeDtypeStruct((b, N_GROUPS, GROUP, GROUP), F32)],
        scratch_shapes=[pltpu.VMEM((nb, N_GROUPS, GROUP, GROUP), F32)],
        compiler_params=pltpu.CompilerParams(dimension_semantics=("arbitrary", "arbitrary"),
                                             vmem_limit_bytes=VMEM_LIMIT_BYTES),
        name="wkv",
    )(r, k, v, kk, beta, lam, s0, tri3, jnp.asarray(ones, BF16), jnp.asarray(ones), eye4)


def _embed_states(s):
    b = s.shape[0]
    eye = jnp.eye(HEADS_PER_GROUP, dtype=s.dtype)
    s = s.reshape(b, N_GROUPS, HEADS_PER_GROUP, HEAD_DIM, HEAD_DIM)
    bd = s[:, :, :, :, None, :] * eye[None, None, :, None, :, None]
    return bd.reshape(b, N_GROUPS, GROUP, GROUP)


def _extract_states(bd):
    b = bd.shape[0]
    x = bd.reshape(b, N_GROUPS, HEADS_PER_GROUP, HEAD_DIM, HEADS_PER_GROUP, HEAD_DIM)
    blocks = [x[:, :, h, :, h, :] for h in range(HEADS_PER_GROUP)]
    return jnp.stack(blocks, axis=2).reshape(b, N_HEADS, HEAD_DIM, HEAD_DIM)


def _cache_rows_kernel(k_ref, v_ref, ko_ref, vo_ref):
    for h in range(N_HEADS):
        ko_ref[0, h] = k_ref[0, :, h * HEAD_DIM:(h + 1) * HEAD_DIM].astype(F32)
        vo_ref[0, h] = v_ref[0, :, h * HEAD_DIM:(h + 1) * HEAD_DIM].astype(F32)


def _cache_rows(p_att, rows):
    b, t, _ = p_att.shape
    assert t % rows == 0
    last = t // rows - 1
    out = jax.ShapeDtypeStruct((b, N_HEADS, rows, HEAD_DIM), F32)
    out_spec = pl.BlockSpec((1, N_HEADS, rows, HEAD_DIM), lambda i: (i, 0, 0, 0))
    return pl.pallas_call(
        _cache_rows_kernel,
        grid=(b,),
        in_specs=[pl.BlockSpec((1, rows, ATT_WIDTH), lambda i: (i, last, 1)),
                  pl.BlockSpec((1, rows, ATT_WIDTH), lambda i: (i, last, 2))],
        out_specs=[out_spec, out_spec],
        out_shape=[out, out],
        compiler_params=pltpu.CompilerParams(dimension_semantics=("arbitrary",)),
        name="cache_rows",
    )(p_att, p_att)


def _heads_first(x):
    b, t, _ = x.shape
    return x.reshape(b, t, N_HEADS, HEAD_DIM).transpose(0, 2, 1, 3)


def kernel(x_prompt, x_sample, state_shift, state_wkv, cache_attn_k, cache_attn_v, norm_ff1, w_ff1_gate, w_ff1_up, w_ff1_down, norm_mix, w_in, mu_shift, w0, w_lora_up, a0, a_lora_up, g_lora_up, k_k, k_a, r_k, ln_x_w, ln_x_b, rel_bias, w_out, norm_ff2, w_ff2_gate, w_ff2_up, w_ff2_down, norm_final):
    assert norm_ff1.shape[0] == 1, "single layer"
    bp, tp, _ = x_prompt.shape
    bs, ts, _ = x_sample.shape
    assert tp % ATT_WINDOW == 0 and ts <= CHUNK and cache_attn_k.shape[3] == ATT_WINDOW

    row = lambda w: w[0].reshape(1, -1).astype(F32)
    bf = lambda w: w[0].astype(BF16)
    ffn1 = (row(norm_ff1), bf(w_ff1_gate), bf(w_ff1_up), bf(w_ff1_down))
    ffn2 = (row(norm_ff2), bf(w_ff2_gate), bf(w_ff2_up), bf(w_ff2_down))
    w_out_b = bf(w_out)
    norm_final_row = norm_final.reshape(1, D_MODEL).astype(F32)
    lora_pad = jnp.zeros((HEAD_DIM, RWKV_WIDTH), BF16)
    gates = (row(mu_shift), row(w0), jnp.concatenate([bf(w_lora_up), lora_pad], axis=0), row(a0),
             jnp.concatenate([lora_pad, bf(a_lora_up)], axis=0), bf(g_lora_up), row(k_k), row(k_a), row(r_k))
    ones = jnp.asarray(_head_block_ones(GROUP), BF16)

    def front(x, first, t_seq):
        return _ffn_inproj(x.reshape(-1, D_MODEL), first, ffn1, row(norm_mix), bf(w_in), gates, ones, t_seq)

    tail = (row(ln_x_w), row(ln_x_b), ones, w_out_b[:RWKV_WIDTH], w_out_b[RWKV_WIDTH:], ffn2, norm_final_row)

    x1, pa, r, k, v, kk, beta, lam, g, bonus, plast = front(x_prompt, jnp.zeros((bp, 1, RWKV_COLS), F32), tp)
    pa = pa.reshape(bp, tp, ATT_COLS)
    at = _band_attn(pa, _rel_bias(rel_bias[0], CHUNK), ATT_WINDOW)
    y_prompt, s_bd = _wkv_out_ffn(x1, bonus, g, at.reshape(bp * tp, ATT_WIDTH), r, k, v, kk, beta, lam,
                                  *tail, tp)
    y_prompt = y_prompt.reshape(bp, tp, D_MODEL)
    p_shift = plast.reshape(bp, -1, RWKV_COLS)[:, -1, :][None]
    p_wkv = _extract_states(s_bd)[None]
    p_k, p_v = _cache_rows(pa, min(ATT_WINDOW, tp))

    x1, pa, r, k, v, kk, beta, lam, g, bonus, plast = front(x_sample, jnp.repeat(state_shift[0], ts, axis=0), ts)
    seq = lambda z: z.reshape(bs, ts, RWKV_WIDTH)
    y, s_bd = _wkv(seq(r), seq(k), seq(v), seq(kk), seq(beta), seq(lam), _embed_states(state_wkv[0]))
    pa = pa.reshape(bs, ts, ATT_COLS)
    q = _heads_first(pa[:, :, :ATT_WIDTH].astype(F32))
    s_k = _heads_first(pa[:, :, ATT_WIDTH:2 * ATT_WIDTH].astype(F32))
    s_v = _heads_first(pa[:, :, 2 * ATT_WIDTH:].astype(F32))
    at = _step_attn(q, cache_attn_k[0], cache_attn_v[0], s_k, s_v, _rel_bias(rel_bias[0], ts))
    at = at.transpose(0, 2, 1, 3).reshape(bs * ts, ATT_WIDTH).astype(BF16)
    y_sample = _out_ffn(x1, y.reshape(bs * ts, RWKV_WIDTH), bonus, g, at, *tail).reshape(bs, ts, D_MODEL)
    s_shift = plast.reshape(bs, ts, RWKV_COLS)[:, -1, :][None]
    s_wkv = _extract_states(s_bd)[None]

    return (y_prompt, y_sample, p_shift, p_wkv, p_k[None], p_v[None], s_shift, s_wkv, s_k[None], s_v[None])
```

```python
import functools
import math

import numpy as np
import jax
import jax.numpy as jnp
from jax import lax
from jax.experimental import pallas as pl
from jax.experimental.pallas import tpu as pltpu

F32 = jnp.float32
BF16 = jnp.bfloat16

D_MODEL = 1024
HEAD_DIM = 64
RWKV_WIDTH = 512
ATT_WIDTH = 512
N_HEADS = 8
LORA_COLS = 256
RWKV_COLS = 3 * RWKV_WIDTH + LORA_COLS
ATT_COLS = 3 * ATT_WIDTH
D_FF = 2816
CHUNK = 64
ATT_WINDOW = 512
REL_MAX = 128
NORM_EPS = 1e-5
GN_EPS = 64e-5
NEG_INF = -1e30

GROUP = 256
HEADS_PER_GROUP = GROUP // HEAD_DIM
N_GROUPS = RWKV_WIDTH // GROUP
KEY_PAD = ATT_WINDOW + 2 * CHUNK
LOG2_E = math.log2(math.e)
QK_SCALE = HEAD_DIM ** -0.5 * LOG2_E

TOKEN_TILE = 512
ROW_PART = 256
WKV_BATCH = 4
WKV_TILE = 128
FF_CHUNK = 256
FUSED_BURST = 8
FUSED_CHAIN_GAP = 2
VMEM_LIMIT_BYTES = 56 * 1024 * 1024
FUSED_VMEM_LIMIT_BYTES = 60 * 1024 * 1024


def _rms(x, g):
    ms = jnp.mean(x * x, axis=-1, keepdims=True)
    return x * lax.rsqrt(ms + NORM_EPS) * g


def _dot(a, b):
    return jnp.dot(a, b, preferred_element_type=F32)


def _dot_nt(a, b):
    return lax.dot_general(a, b, (((1,), (1,)), ((), ())), preferred_element_type=F32)


def _row_parts(tm):
    n = max(1, tm // ROW_PART)
    return [slice(i * (tm // n), (i + 1) * (tm // n)) for i in range(n)]


def _head_sums(x, ones_ref):
    return jnp.concatenate([_dot(x[:, g * GROUP:(g + 1) * GROUP], ones_ref[...]) for g in range(N_GROUPS)],
                           axis=1)


def _const_spec(shape):
    return pl.BlockSpec(shape, lambda *_: (0,) * len(shape), pipeline_mode=pl.Buffered(1))


def _head_block_ones(n):
    h = np.arange(n) // HEAD_DIM
    return (h[:, None] == h[None, :]).astype(np.float32)


def _ffn_inproj_kernel(x_ref, first_ref, g1_ref, wg_ref, wu_ref, wd_ref, gm_ref, win_ref,
                       mu_ref, w0_ref, wl_ref, a0_ref, al_ref, gl_ref, kkw_ref, ka_ref, rk_ref, ones_ref,
                       x1_ref, pa_ref, r_ref, k_ref, v_ref, kk_ref, beta_ref, lam_ref, g_ref, bonus_ref,
                       plast_ref, carry_scr, *, tm, t_seq):
    parts = _row_parts(tm)
    rows = parts[0].stop
    q_cols = slice(RWKV_COLS, RWKV_COLS + ATT_WIDTH)

    def gate_up(rs):
        h = _rms(x_ref[rs, :], g1_ref[...]).astype(BF16)
        return _dot(h, wg_ref[...]), _dot(h, wu_ref[...])

    def down_inproj(rs, gate, up):
        act = (gate * jax.nn.sigmoid(gate) * up).astype(BF16)
        x1 = x_ref[rs, :] + 0.5 * _dot(act, wd_ref[...])
        x1_ref[rs, :] = x1
        h2 = _rms(x1, gm_ref[...]).astype(BF16)
        p = _dot(h2, win_ref[:, :RWKV_COLS])
        pa_ref[rs, :ATT_WIDTH] = (_dot(h2, win_ref[:, q_cols]) * QK_SCALE).astype(BF16)
        pa_ref[rs, ATT_WIDTH:] = _dot(h2, win_ref[:, RWKV_COLS + ATT_WIDTH:]).astype(BF16)
        return p

    def gates(rs, p, before):
        row = lax.broadcasted_iota(jnp.int32, (rows, RWKV_COLS), 0)
        first_row = (row == 0) if t_seq >= tm else ((row & (t_seq - 1)) == 0)
        xs = p + (jnp.where(first_row, before, pltpu.roll(p, 1, axis=0)) - p) * mu_ref[...]
        r = xs[:, 0:RWKV_WIDTH]
        k = xs[:, RWKV_WIDTH:2 * RWKV_WIDTH]
        v = xs[:, 2 * RWKV_WIDTH:3 * RWKV_WIDTH]
        lora_in = xs[:, 3 * RWKV_WIDTH:3 * RWKV_WIDTH + 128]
        gate_in = xs[:, 3 * RWKV_WIDTH + 128:]
        z = w0_ref[...] + _dot(jnp.tanh(lora_in).astype(BF16), wl_ref[...])
        a = jax.nn.sigmoid(a0_ref[...] + _dot(lora_in.astype(BF16), al_ref[...]))
        g_ref[rs, :] = _dot(jax.nn.sigmoid(gate_in).astype(BF16), gl_ref[...]).astype(BF16)
        kk = k * kkw_ref[...]
        kk = kk * lax.rsqrt(jnp.maximum(_head_sums((kk * kk).astype(BF16), ones_ref), 1e-24))
        k = k * (1.0 + (a - 1.0) * ka_ref[...])
        lam_ref[rs, :] = -math.exp(-0.5) * jax.nn.sigmoid(z)
        r_ref[rs, :] = r.astype(BF16)
        k_ref[rs, :] = k.astype(BF16)
        v_ref[rs, :] = v.astype(BF16)
        kk_ref[rs, :] = kk.astype(BF16)
        beta_ref[rs, :] = (a * kk).astype(BF16)
        bonus_ref[rs, :] = (_head_sums((r * k * rk_ref[...]).astype(BF16), ones_ref) * v).astype(BF16)

    if t_seq >= tm:
        @pl.when(pl.program_id(0) % (t_seq // tm) == 0)
        def _():
            carry_scr[...] = first_ref[0]
        before = carry_scr[...]
    else:
        assert len(parts) == 1
        before = first_ref[...]

    gu = gate_up(parts[0])
    for i, rs in enumerate(parts):
        p = down_inproj(rs, *gu)
        if i + 1 < len(parts):
            gu = gate_up(parts[i + 1])
        gates(rs, p, before)
        before = p[rows - 1:rows, :]
    if t_seq >= tm:
        carry_scr[...] = before
        plast_ref[0] = before
    else:
        plast_ref[...] = p


def _ffn_inproj(x, first, ffn, gm, win, gates, ones, t_seq):
    n = x.shape[0]
    tm = min(TOKEN_TILE, n)
    assert n % tm == 0 and (t_seq % tm == 0 or tm % t_seq == 0) and t_seq & (t_seq - 1) == 0
    row = lambda i: (i, 0)
    vec = lambda m: _const_spec((1, m))
    wide = lambda dt: jax.ShapeDtypeStruct((n, RWKV_WIDTH), dt)
    wide_spec = pl.BlockSpec((tm, RWKV_WIDTH), row)
    if t_seq >= tm:
        first_spec = pl.BlockSpec((1, 1, RWKV_COLS), lambda i: (i // (t_seq // tm), 0, 0))
        plast_shape = jax.ShapeDtypeStruct((n // tm, 1, RWKV_COLS), F32)
        plast_spec = pl.BlockSpec((1, 1, RWKV_COLS), lambda i: (i, 0, 0))
    else:
        first_spec = pl.BlockSpec((tm, RWKV_COLS), row)
        plast_shape = jax.ShapeDtypeStruct((n, RWKV_COLS), F32)
        plast_spec = pl.BlockSpec((tm, RWKV_COLS), row)
    return pl.pallas_call(
        functools.partial(_ffn_inproj_kernel, tm=tm, t_seq=t_seq),
        grid=(n // tm,),
        in_specs=[pl.BlockSpec((tm, D_MODEL), row), first_spec,
                  vec(D_MODEL), _const_spec((D_MODEL, D_FF)), _const_spec((D_MODEL, D_FF)),
                  _const_spec((D_FF, D_MODEL)), vec(D_MODEL), _const_spec((D_MODEL, RWKV_COLS + ATT_COLS)),
                  vec(RWKV_COLS), vec(RWKV_WIDTH), _const_spec((128, RWKV_WIDTH)), vec(RWKV_WIDTH),
                  _const_spec((128, RWKV_WIDTH)), _const_spec((128, RWKV_WIDTH)),
                  vec(RWKV_WIDTH), vec(RWKV_WIDTH), vec(RWKV_WIDTH), _const_spec((GROUP, GROUP))],
        out_specs=[pl.BlockSpec((tm, D_MODEL), row), pl.BlockSpec((tm, ATT_COLS), row)]
                  + [wide_spec] * 8 + [plast_spec],
        out_shape=[jax.ShapeDtypeStruct((n, D_MODEL), F32), jax.ShapeDtypeStruct((n, ATT_COLS), BF16),
                   wide(BF16), wide(BF16), wide(BF16), wide(BF16), wide(BF16), wide(F32), wide(BF16),
                   wide(BF16), plast_shape],
        scratch_shapes=[pltpu.VMEM((1, RWKV_COLS), F32)],
        compiler_params=pltpu.CompilerParams(dimension_semantics=("arbitrary",),
                                             vmem_limit_bytes=VMEM_LIMIT_BYTES),
        name="ffn_inproj",
    )(x, first, *ffn, gm, win, *gates, ones)


def _out_ffn_kernel(x1_ref, y_ref, bonus_ref, g_ref, at_ref, lnw_ref, lnb_ref, ones_ref, wor_ref, woa_ref,
                    g2_ref, wg_ref, wu_ref, wd_ref, gf_ref, o_ref):
    parts = _row_parts(x1_ref.shape[0])
    nparts = range(len(parts))
    inv_n = 1.0 / HEAD_DIM
    mean = [_head_sums(y_ref[rs, :], ones_ref) * inv_n for rs in parts]
    d = [y_ref[parts[i], :].astype(F32) - mean[i] for i in nparts]
    var = [_head_sums((t * t).astype(BF16), ones_ref) * inv_n for t in d]
    ro = []
    for i in nparts:
        rs = parts[i]
        yn = d[i] * lax.rsqrt(var[i] + GN_EPS) * lnw_ref[...] + lnb_ref[...]
        ro.append(((yn + bonus_ref[rs, :].astype(F32)) * g_ref[rs, :].astype(F32)).astype(BF16))
    x2 = [x1_ref[parts[i], :] + _dot(ro[i], wor_ref[...]) + _dot(at_ref[parts[i], :], woa_ref[...])
          for i in nparts]
    h = [_rms(t, g2_ref[...]).astype(BF16) for t in x2]
    gu = [(_dot(hi, wg_ref[...]), _dot(hi, wu_ref[...])) for hi in h]
    act = [(gate * jax.nn.sigmoid(gate) * up).astype(BF16) for gate, up in gu]
    x3 = [x2[i] + 0.5 * _dot(act[i], wd_ref[...]) for i in nparts]
    for i in nparts:
        o_ref[parts[i], :] = _rms(x3[i], gf_ref[...])


def _out_ffn(x1, y, bonus, g, at, lnw, lnb, ones, wor, woa, ffn, gf):
    n = x1.shape[0]
    tm = min(TOKEN_TILE, n)
    row = lambda i: (i, 0)
    vec = lambda m: _const_spec((1, m))
    wide_spec = pl.BlockSpec((tm, RWKV_WIDTH), row)
    return pl.pallas_call(
        _out_ffn_kernel,
        grid=(n // tm,),
        in_specs=[pl.BlockSpec((tm, D_MODEL), row), wide_spec, wide_spec, wide_spec, wide_spec,
                  vec(RWKV_WIDTH), vec(RWKV_WIDTH), _const_spec((GROUP, GROUP)),
                  _const_spec((RWKV_WIDTH, D_MODEL)), _const_spec((ATT_WIDTH, D_MODEL)),
                  vec(D_MODEL), _const_spec((D_MODEL, D_FF)), _const_spec((D_MODEL, D_FF)),
                  _const_spec((D_FF, D_MODEL)), vec(D_MODEL)],
        out_specs=pl.BlockSpec((tm, D_MODEL), row),
        out_shape=jax.ShapeDtypeStruct((n, D_MODEL), F32),
        compiler_params=pltpu.CompilerParams(dimension_semantics=("arbitrary",),
                                             vmem_limit_bytes=VMEM_LIMIT_BYTES),
        name="out_ffn",
    )(x1, y, bonus, g, at, lnw, lnb, ones, wor, woa, *ffn, gf)


def _bias_kernel(tab_ref, o_ref, *, cq):
    qi = lax.broadcasted_iota(jnp.int32, (cq, KEY_PAD), 0)
    kj = lax.broadcasted_iota(jnp.int32, (cq, KEY_PAD), 1)
    idx = jnp.clip(kj - ATT_WINDOW - qi, -REL_MAX, REL_MAX) + REL_MAX
    pad = kj >= ATT_WINDOW + cq
    for h in range(N_HEADS):
        def body(t, acc):
            return jnp.where(idx == t, tab_ref[h, t], acc)
        acc = lax.fori_loop(0, REL_MAX + cq, body, jnp.zeros((cq, KEY_PAD), F32))
        o_ref[h * cq:(h + 1) * cq, :] = jnp.where(pad, NEG_INF, acc * LOG2_E)


def _rel_bias(table, cq):
    return pl.pallas_call(
        functools.partial(_bias_kernel, cq=cq),
        in_specs=[pl.BlockSpec(memory_space=pltpu.SMEM)],
        out_specs=pl.BlockSpec(memory_space=pltpu.VMEM),
        out_shape=jax.ShapeDtypeStruct((N_HEADS * cq, KEY_PAD), F32),
        name=f"rel_bias_{cq}",
    )(table)


def _head_masks(rows, dtype):
    lane_head = lax.broadcasted_iota(jnp.int32, (rows, GROUP), 1) >> 6
    return [(lane_head == h).astype(F32).astype(dtype) for h in range(HEADS_PER_GROUP)]


def _block_diag_rows(x, masks):
    return jnp.concatenate([x * m for m in masks], axis=0)


def _band_attn_kernel(q_ref, kp_ref, kc_ref, vp_ref, vc_ref, bias_ref, o_ref, kbuf, vbuf, *, tq):
    kbuf[0:tq, :] = kp_ref[0]
    kbuf[tq:2 * tq, :] = kc_ref[0]
    kbuf[2 * tq:, :] = jnp.zeros((CHUNK, ATT_WIDTH), BF16)
    vbuf[0:tq, :] = vp_ref[0]
    vbuf[tq:2 * tq, :] = vc_ref[0]
    vbuf[2 * tq:, :] = jnp.zeros((CHUNK, ATT_WIDTH), BF16)
    masks_b = _head_masks(CHUNK, BF16)
    masks_f = _head_masks(CHUNK, F32)
    rows = HEADS_PER_GROUP * CHUNK
    cells = [(j, g) for j in range(tq // CHUNK) for g in range(N_GROUPS)]
    lanes = [slice(g * GROUP, (g + 1) * GROUP) for g in range(N_GROUPS)]
    win = [slice(j * CHUNK + tq - ATT_WINDOW, j * CHUNK + tq - ATT_WINDOW + KEY_PAD)
           for j in range(tq // CHUNK)]

    def run(first_tile):
        def scores(j, g):
            lhs = _block_diag_rows(q_ref[0, j * CHUNK:(j + 1) * CHUNK, lanes[g]], masks_b)
            return _dot_nt(lhs, kbuf[win[j], lanes[g]])

        def softmax(j, g, s):
            s = s + bias_ref[g * rows:(g + 1) * rows, :]
            if first_tile:
                kj = lax.broadcasted_iota(jnp.int32, (rows, KEY_PAD), 1)
                s = jnp.where(kj >= tq - j * CHUNK, s, NEG_INF)
            p = jnp.exp2(s - jnp.max(s, axis=-1, keepdims=True))
            return p.astype(BF16), 1.0 / jnp.sum(p, axis=-1, keepdims=True)

        def values(j, g, p, inv):
            o_full = _dot(p, vbuf[win[j], lanes[g]]) * inv
            o = o_full[0:CHUNK] * masks_f[0]
            for h in range(1, HEADS_PER_GROUP):
                o = o + o_full[h * CHUNK:(h + 1) * CHUNK] * masks_f[h]
            return o

        s, pz, o = {}, {}, {}
        for n in range(len(cells) + 2):
            if n < len(cells):
                s[n] = scores(*cells[n])
            if 1 <= n <= len(cells):
                pz[n - 1] = softmax(*cells[n - 1], s.pop(n - 1))
            if n >= 2:
                j, g = cells[n - 2]
                o[g] = values(j, g, *pz.pop(n - 2))
                if g == N_GROUPS - 1:
                    o_ref[0, j * CHUNK:(j + 1) * CHUNK, :] = jnp.concatenate(
                        [o[gg] for gg in range(N_GROUPS)], axis=1).astype(BF16)

    @pl.when(pl.program_id(1) == 0)
    def _():
        run(True)

    @pl.when(pl.program_id(1) > 0)
    def _():
        run(False)


def _band_attn(p_att, bias, tq):
    b, t, _ = p_att.shape
    assert tq == ATT_WINDOW and t % tq == 0
    cur = lambda col: (lambda bi, i: (bi, i, col))
    prev = lambda col: (lambda bi, i: (bi, jnp.maximum(i - 1, 0), col))
    blk = (1, tq, ATT_WIDTH)
    return pl.pallas_call(
        functools.partial(_band_attn_kernel, tq=tq),
        grid=(b, t // tq),
        in_specs=[pl.BlockSpec(blk, cur(0)), pl.BlockSpec(blk, prev(1)), pl.BlockSpec(blk, cur(1)),
                  pl.BlockSpec(blk, prev(2)), pl.BlockSpec(blk, cur(2)),
                  _const_spec((N_HEADS * CHUNK, KEY_PAD))],
        out_specs=pl.BlockSpec(blk, cur(0)),
        out_shape=jax.ShapeDtypeStruct((b, t, ATT_WIDTH), BF16),
        scratch_shapes=[pltpu.VMEM((2 * tq + CHUNK, ATT_WIDTH), BF16),
                        pltpu.VMEM((2 * tq + CHUNK, ATT_WIDTH), BF16)],
        compiler_params=pltpu.CompilerParams(dimension_semantics=("arbitrary", "arbitrary"),
                                             vmem_limit_bytes=VMEM_LIMIT_BYTES),
        name="band_attn",
    )(p_att, p_att, p_att, p_att, p_att, bias)


def _step_attn_kernel(q_ref, kc_ref, vc_ref, kn_ref, vn_ref, bias_ref, o_ref, *, tn):
    heads = range(N_HEADS)
    q = [q_ref[0, h].astype(BF16) for h in heads]
    s1 = [_dot_nt(q[h], kc_ref[0, h].astype(BF16)) for h in heads]
    s2 = [_dot_nt(q[h], kn_ref[0, h].astype(BF16)) for h in heads]
    p1, p2, inv = [], [], []
    for h in heads:
        b = bias_ref[h * tn:(h + 1) * tn, :]
        a1 = s1[h] + b[:, :ATT_WINDOW]
        a2 = s2[h] + b[:, ATT_WINDOW:ATT_WINDOW + tn]
        m = jnp.maximum(jnp.max(a1, axis=-1, keepdims=True), jnp.max(a2, axis=-1, keepdims=True))
        e1 = jnp.exp2(a1 - m)
        e2 = jnp.exp2(a2 - m)
        inv.append(1.0 / (jnp.sum(e1, axis=-1, keepdims=True) + jnp.sum(e2, axis=-1, keepdims=True)))
        p1.append(e1.astype(BF16))
        p2.append(e2.astype(BF16))
    o1 = [_dot(p1[h], vc_ref[0, h].astype(BF16)) for h in heads]
    o2 = [_dot(p2[h], vn_ref[0, h].astype(BF16)) for h in heads]
    for h in heads:
        o_ref[0, h] = (o1[h] + o2[h]) * inv[h]


def _step_attn(q, k_cache, v_cache, k_new, v_new, bias):
    b, _, tn, _ = q.shape
    new = pl.BlockSpec((1, N_HEADS, tn, HEAD_DIM), lambda i: (i, 0, 0, 0))
    old = pl.BlockSpec((1, N_HEADS, ATT_WINDOW, HEAD_DIM), lambda i: (i, 0, 0, 0))
    return pl.pallas_call(
        functools.partial(_step_attn_kernel, tn=tn),
        grid=(b,),
        in_specs=[new, old, old, new, new, _const_spec((N_HEADS * tn, KEY_PAD))],
        out_specs=new,
        out_shape=jax.ShapeDtypeStruct((b, N_HEADS, tn, HEAD_DIM), F32),
        compiler_params=pltpu.CompilerParams(dimension_semantics=("arbitrary",)),
        name="step_attn",
    )(q, k_cache, v_cache, k_new, v_new, bias)


def _split3(x):
    hi = x.astype(BF16)
    r1 = x - hi.astype(F32)
    mid = r1.astype(BF16)
    lo = (r1 - mid.astype(F32)).astype(BF16)
    return hi, mid, lo


def _wkv_stages(load, state, tri3, hm, bdm, eye4, lanes_of, n_chunks, burst=None, chain_gap=1):
    c = CHUNK
    burst = burst or len(lanes_of) * n_chunks
    t_row = lax.broadcasted_iota(jnp.int32, (c, GROUP), 0)
    s_col = lax.broadcasted_iota(jnp.int32, (c, GROUP), 1) & (c - 1)
    strict = s_col < t_row
    incl = s_col <= t_row
    zero = jnp.zeros((), F32)

    def bd(x):
        return jnp.concatenate([x] * HEADS_PER_GROUP, axis=0) * hm

    cells = [(lane, ci) for ci in range(n_chunks) for lane in lanes_of]

    def each(fn):
        out = {}
        for n, z in enumerate(cells):
            out[z] = fn(z)
            if (n + 1) % burst == 0:
                yield 1
        if len(cells) % burst:
            yield 1
        return out

    lam = {z: load("lam", *z) for z in cells}
    cum = yield from each(lambda z: _dot(tri3, jnp.concatenate(_split3(lam[z]), axis=0)))
    tot = {z: cum[z][c - 1:c, :] for z in cells}
    lhs, bt, kt, bk, v_bd = {}, {}, {}, {}, {}
    for z in cells:
        r, k, kk, beta = (load(name, *z).astype(F32) for name in ("r", "k", "kk", "beta"))
        e_neg = jnp.exp(-cum[z])
        e_rem = jnp.exp(tot[z] - cum[z])
        lhs[z] = jnp.concatenate([-kk * jnp.exp(cum[z] - lam[z]), r * jnp.exp(cum[z])],
                                 axis=0).astype(BF16)
        bt[z] = bd((beta * e_neg).astype(BF16))
        kt[z] = bd((k * e_neg).astype(BF16))
        bk[z] = jnp.concatenate([beta * e_rem, k * e_rem], axis=0).astype(BF16)
        v_bd[z] = bd(load("v", *z))
    ab = yield from each(lambda z: _dot_nt(lhs[z], bt[z]))
    ak = yield from each(lambda z: _dot_nt(lhs[z], kt[z]))
    a_rb = {z: jnp.where(incl, ab[z][c:], zero).astype(BF16) for z in cells}
    a_ak = {z: jnp.where(strict, ak[z][:c], zero).astype(BF16) for z in cells}
    a_rk = {z: jnp.where(incl, ak[z][c:], zero).astype(BF16) for z in cells}
    n_k = {z: jnp.where(strict, ab[z][:c], zero) for z in cells}
    t_m = {z: eye4 + n_k[z] for z in cells}
    n_b = {z: n_k[z].astype(BF16) for z in cells}
    n_b = yield from each(lambda z: _dot(n_b[z], bd(n_b[z])).astype(BF16))
    for _ in range(int(math.log2(c)) - 2):
        both = yield from each(
            lambda z: _dot(jnp.concatenate([t_m[z].astype(BF16), n_b[z]], axis=0), bd(n_b[z])))
        t_m = {z: t_m[z] + both[z][:c] for z in cells}
        n_b = {z: both[z][c:].astype(BF16) for z in cells}
    t_b = yield from each(lambda z: (t_m[z] + _dot(t_m[z].astype(BF16), bd(n_b[z]))).astype(BF16))
    wy_v = yield from each(lambda z: _dot(jnp.concatenate([a_ak[z], a_rk[z]], axis=0), v_bd[z]))
    w_v = {z: wy_v[z][:c] for z in cells}
    y_v = {z: wy_v[z][c:] for z in cells}

    def outputs(zs, x, u):
        return [(z[0], z[1], x[z][c:] + y_v[z] + _dot(a_rb[z], bd(u[z].astype(BF16)))) for z in zs]

    pending = None
    for ci in range(n_chunks):
        zs = [(lane, ci) for lane in lanes_of]
        x = {z: _dot_nt(lhs[z], state[z[0]].astype(BF16)) for z in zs}
        if pending is not None:
            yield from outputs(*pending)
        yield chain_gap
        u = {z: _dot(t_b[z], bd((x[z][:c] + w_v[z]).astype(BF16))) for z in zs}
        yield chain_gap
        for z in zs:
            uv_t = jnp.concatenate([u[z], load("v", *z).astype(F32)], axis=0).T.astype(BF16)
            state[z[0]] = state[z[0]] * jnp.exp(tot[z]) + _dot(uv_t, bk[z]) * bdm
        yield chain_gap
        pending = (zs, x, u)
    yield from outputs(*pending)


def _store_head_states(out_ref, out_idx, s_scr, scr_idx):
    for g in range(N_GROUPS):
        s = s_scr[g] if scr_idx is None else s_scr[scr_idx, g]
        for h in range(HEADS_PER_GROUP):
            blk = slice(h * HEAD_DIM, (h + 1) * HEAD_DIM)
            out_ref[out_idx, g * HEADS_PER_GROUP + h] = s[blk, blk]


def _wkv_kernel(r_ref, k_ref, v_ref, kk_ref, beta_ref, lam_ref, s0_ref, tri3_ref, hm_ref, bdm_ref, eye4_ref,
                y_ref, s_out_ref, s_scr, *, nb, tt):
    ti = pl.program_id(1)

    @pl.when(ti == 0)
    def _():
        spread = eye4_ref[...].astype(BF16)
        for b in range(nb):
            for g in range(N_GROUPS):
                heads = jnp.concatenate([s0_ref[b, g * HEADS_PER_GROUP + h] for h in range(HEADS_PER_GROUP)],
                                        axis=0)
                hi, mid, lo = _split3(heads)
                s_scr[b, g] = (_dot(hi, spread) + _dot(mid, spread) + _dot(lo, spread)) * bdm_ref[...]

    c = CHUNK
    n_chunks = -(-tt // c)
    pad = n_chunks * c - tt
    refs = dict(r=r_ref, k=k_ref, v=v_ref, kk=kk_ref, beta=beta_ref, lam=lam_ref)

    def load(name, lane, ci):
        b, g = lane
        lanes = slice(g * GROUP, (g + 1) * GROUP)
        if pad == 0:
            return refs[name][b, ci * c:(ci + 1) * c, lanes]
        x = refs[name][b, :, lanes]
        return jnp.concatenate([x, jnp.zeros((pad, GROUP), x.dtype)], axis=0)

    lanes_of = [(b, g) for b in range(nb) for g in range(N_GROUPS)]
    state = {lane: s_scr[lane[0], lane[1]] for lane in lanes_of}
    y = {}
    for out in _wkv_stages(load, state, tri3_ref[...], hm_ref[...], bdm_ref[...], eye4_ref[...],
                           lanes_of, n_chunks):
        if not isinstance(out, int):
            lane, ci, y_c = out
            y[(lane, ci)] = y_c
    for lane in lanes_of:
        s_scr[lane[0], lane[1]] = state[lane]
    for b in range(nb):
        rows = [jnp.concatenate([y[((b, g), ci)] for g in range(N_GROUPS)], axis=1) for ci in range(n_chunks)]
        y_b = rows[0] if n_chunks == 1 else jnp.concatenate(rows, axis=0)
        y_ref[b] = y_b[:tt].astype(BF16)

    @pl.when(ti == pl.num_programs(1) - 1)
    def _():
        for b in range(nb):
            _store_head_states(s_out_ref, b, s_scr, b)


def _alternate(main, side):
    results = []
    for out in side:
        if isinstance(out, int):
            for _ in range(out):
                next(main, _DONE)
        else:
            results.append(out)
    for _ in main:
        pass
    return results


_DONE = object()


def _wkv_out_ffn_kernel(x1_ref, bonus_ref, g_ref, at_ref, r_ref, k_ref, v_ref, kk_ref, beta_ref, lam_ref,
                        lnw_ref, lnb_ref, ones_ref, wor_ref, woa_ref, g2_ref, wg_ref, wu_ref, wd_ref, gf_ref,
                        tri3_ref, bdm_ref, eye4_ref, o_ref, s_out_ref, y_scr, s_scr, *, tm, tiles_per_seq):
    i = pl.program_id(0)
    n_tiles = pl.num_programs(0) - 1
    tile = jnp.minimum(i, n_tiles - 1)

    @pl.when(i == 0)
    def _():
        y_scr[...] = jnp.zeros_like(y_scr)

    @pl.when(tile % tiles_per_seq == 0)
    def _():
        s_scr[...] = jnp.zeros_like(s_scr)

    parts = _row_parts(tm)
    inv_n = 1.0 / HEAD_DIM
    y_prev = [y_scr[rs, :] for rs in parts]
    ff_cols = [slice(j, min(j + FF_CHUNK, D_FF)) for j in range(0, D_FF, FF_CHUNK)]

    def ffn_stages():
        x2, h = [], []
        for i_p, rs in enumerate(parts):
            mean = _head_sums(y_prev[i_p], ones_ref) * inv_n
            yield
            d = y_prev[i_p].astype(F32) - mean
            var = _head_sums((d * d).astype(BF16), ones_ref) * inv_n
            yield
            yn = d * lax.rsqrt(var + GN_EPS) * lnw_ref[...] + lnb_ref[...]
            ro = ((yn + bonus_ref[rs, :].astype(F32)) * g_ref[rs, :].astype(F32)).astype(BF16)
            x2.append(x1_ref[rs, :] + _dot(ro, wor_ref[...]) + _dot(at_ref[rs, :], woa_ref[...]))
            yield
            h.append(_rms(x2[i_p], g2_ref[...]).astype(BF16))
        acc = [None] * len(parts)
        for cols in ff_cols:
            gu = []
            for i_p in range(len(parts)):
                gate = _dot(h[i_p], wg_ref[:, cols])
                yield
                up = _dot(h[i_p], wu_ref[:, cols])
                yield
                gu.append((gate, up))
            for i_p, (gate, up) in enumerate(gu):
                act = (gate * jax.nn.sigmoid(gate) * up).astype(BF16)
                down = _dot(act, wd_ref[cols, :])
                acc[i_p] = down if acc[i_p] is None else acc[i_p] + down
                yield
        for i_p, rs in enumerate(parts):
            o_ref[rs, :] = _rms(x2[i_p] + 0.5 * acc[i_p], gf_ref[...])

    c = CHUNK
    refs = dict(r=r_ref, k=k_ref, v=v_ref, kk=kk_ref, beta=beta_ref, lam=lam_ref)

    def load(name, g, ci):
        return refs[name][ci * c:(ci + 1) * c, g * GROUP:(g + 1) * GROUP]

    lanes_of = list(range(N_GROUPS))
    state = {g: s_scr[g] for g in lanes_of}
    wkv = _wkv_stages(load, state, tri3_ref[...], ones_ref[...], bdm_ref[...], eye4_ref[...], lanes_of, tm // c,
                      burst=FUSED_BURST, chain_gap=FUSED_CHAIN_GAP)
    for g, ci, y_c in _alternate(ffn_stages(), wkv):
        y_scr[ci * c:(ci + 1) * c, g * GROUP:(g + 1) * GROUP] = y_c.astype(BF16)
    for g in lanes_of:
        s_scr[g] = state[g]

    @pl.when(i < n_tiles)
    def _():
        _store_head_states(s_out_ref, 0, s_scr, None)


def _wkv_out_ffn(x1, bonus, g, at, r, k, v, kk, beta, lam, lnw, lnb, ones, wor, woa, ffn, gf, t_seq):
    n = x1.shape[0]
    tm = TOKEN_TILE
    assert n % tm == 0 and t_seq % tm == 0
    n_tiles = n // tm
    c = CHUNK
    tri = np.tril(np.ones((c, c), np.float32))
    tri3 = jnp.asarray(np.concatenate([tri, tri, tri], axis=1), BF16)
    bdm = jnp.asarray(_head_block_ones(GROUP))
    eye4 = jnp.asarray(np.tile(np.eye(c, dtype=np.float32), (1, HEADS_PER_GROUP)))
    prev = lambda i: (jnp.maximum(i - 1, 0), 0)
    cur = lambda i: (jnp.minimum(i, n_tiles - 1), 0)
    vec = lambda m: _const_spec((1, m))
    wide_prev = pl.BlockSpec((tm, RWKV_WIDTH), prev)
    wide_cur = pl.BlockSpec((tm, RWKV_WIDTH), cur)
    state = pl.BlockSpec((1, N_HEADS, HEAD_DIM, HEAD_DIM),
                         lambda i: (jnp.minimum(i, n_tiles - 1) // (t_seq // tm), 0, 0, 0))
    return pl.pallas_call(
        functools.partial(_wkv_out_ffn_kernel, tm=tm, tiles_per_seq=t_seq // tm),
        grid=(n_tiles + 1,),
        in_specs=[pl.BlockSpec((tm, D_MODEL), prev), wide_prev, wide_prev, wide_prev,
                  wide_cur, wide_cur, wide_cur, wide_cur, wide_cur, wide_cur,
                  vec(RWKV_WIDTH), vec(RWKV_WIDTH), _const_spec((GROUP, GROUP)),
                  _const_spec((RWKV_WIDTH, D_MODEL)), _const_spec((ATT_WIDTH, D_MODEL)),
                  vec(D_MODEL), _const_spec((D_MODEL, D_FF)), _const_spec((D_MODEL, D_FF)),
                  _const_spec((D_FF, D_MODEL)), vec(D_MODEL),
                  _const_spec((c, 3 * c)), _const_spec((GROUP, GROUP)), _const_spec((c, GROUP))],
        out_specs=[pl.BlockSpec((tm, D_MODEL), prev), state],
        out_shape=[jax.ShapeDtypeStruct((n, D_MODEL), F32),
                   jax.ShapeDtypeStruct((n // t_seq, N_HEADS, HEAD_DIM, HEAD_DIM), F32)],
        scratch_shapes=[pltpu.VMEM((tm, RWKV_WIDTH), BF16), pltpu.VMEM((N_GROUPS, GROUP, GROUP), F32)],
        compiler_params=pltpu.CompilerParams(dimension_semantics=("arbitrary",),
                                             vmem_limit_bytes=FUSED_VMEM_LIMIT_BYTES),
        name="wkv_out_ffn",
    )(x1, bonus, g, at, r, k, v, kk, beta, lam, lnw, lnb, ones, wor, woa, *ffn, gf, tri3, bdm, eye4)


def _wkv(r, k, v, kk, beta, lam, s0):
    b, t, _ = r.shape
    nb = WKV_BATCH if b % WKV_BATCH == 0 else 1
    tt = min(WKV_TILE, t)
    assert t % tt == 0
    c = CHUNK
    tri = np.tril(np.ones((c, c), np.float32))
    tri3 = jnp.asarray(np.concatenate([tri, tri, tri], axis=1), BF16)
    ones = _head_block_ones(GROUP)
    eye4 = jnp.asarray(np.tile(np.eye(c, dtype=np.float32), (1, HEADS_PER_GROUP)))
    tok = pl.BlockSpec((nb, tt, RWKV_WIDTH), lambda bi, i: (bi, i, 0))
    state = pl.BlockSpec((nb, N_HEADS, HEAD_DIM, HEAD_DIM), lambda bi, i: (bi, 0, 0, 0))
    return pl.pallas_call(
        functools.partial(_wkv_kernel, nb=nb, tt=tt),
        grid=(b // nb, t // tt),
        in_specs=[tok, tok, tok, tok, tok, tok, state,
                  _const_spec((c, 3 * c)), _const_spec((GROUP, GROUP)), _const_spec((GROUP, GROUP)),
                  _const_spec((c, GROUP))],
        out_specs=[tok, state],
        out_shape=[jax.ShapeDtypeStruct((b, t, RWKV_WIDTH), BF16),
                   jax.ShapeDtypeStruct((b, N_HEADS, HEAD_DIM, HEAD_DIM), F32)],
        scratch_shapes=[pltpu.VMEM((nb, N_GROUPS, GROUP, GROUP), F32)],
        compiler_params=pltpu.CompilerParams(dimension_semantics=("arbitrary", "arbitrary"),
                                             vmem_limit_bytes=VMEM_LIMIT_BYTES),
        name="wkv",
    )(r, k, v, kk, beta, lam, s0, tri3, jnp.asarray(ones, BF16), jnp.asarray(ones), eye4)


def _cache_rows_kernel(k_ref, v_ref, ko_ref, vo_ref):
    for h in range(N_HEADS):
        ko_ref[0, h] = k_ref[0, :, h * HEAD_DIM:(h + 1) * HEAD_DIM].astype(F32)
        vo_ref[0, h] = v_ref[0, :, h * HEAD_DIM:(h + 1) * HEAD_DIM].astype(F32)


def _cache_rows(p_att, rows):
    b, t, _ = p_att.shape
    assert t % rows == 0
    last = t // rows - 1
    out = jax.ShapeDtypeStruct((b, N_HEADS, rows, HEAD_DIM), F32)
    out_spec = pl.BlockSpec((1, N_HEADS, rows, HEAD_DIM), lambda i: (i, 0, 0, 0))
    return pl.pallas_call(
        _cache_rows_kernel,
        grid=(b,),
        in_specs=[pl.BlockSpec((1, rows, ATT_WIDTH), lambda i: (i, last, 1)),
                  pl.BlockSpec((1, rows, ATT_WIDTH), lambda i: (i, last, 2))],
        out_specs=[out_spec, out_spec],
        out_shape=[out, out],
        compiler_params=pltpu.CompilerParams(dimension_semantics=("arbitrary",)),
        name="cache_rows",
    )(p_att, p_att)


def _heads_first(x):
    b, t, _ = x.shape
    return x.reshape(b, t, N_HEADS, HEAD_DIM).transpose(0, 2, 1, 3)


def kernel(x_prompt, x_sample, state_shift, state_wkv, cache_attn_k, cache_attn_v, norm_ff1, w_ff1_gate, w_ff1_up, w_ff1_down, norm_mix, w_in, mu_shift, w0, w_lora_up, a0, a_lora_up, g_lora_up, k_k, k_a, r_k, ln_x_w, ln_x_b, rel_bias, w_out, norm_ff2, w_ff2_gate, w_ff2_up, w_ff2_down, norm_final):
    assert norm_ff1.shape[0] == 1, "single layer"
    bp, tp, _ = x_prompt.shape
    bs, ts, _ = x_sample.shape
    assert tp % ATT_WINDOW == 0 and ts <= CHUNK and cache_attn_k.shape[3] == ATT_WINDOW

    row = lambda w: w[0].reshape(1, -1).astype(F32)
    bf = lambda w: w[0].astype(BF16)
    ffn1 = (row(norm_ff1), bf(w_ff1_gate), bf(w_ff1_up), bf(w_ff1_down))
    ffn2 = (row(norm_ff2), bf(w_ff2_gate), bf(w_ff2_up), bf(w_ff2_down))
    w_out_b = bf(w_out)
    norm_final_row = norm_final.reshape(1, D_MODEL).astype(F32)
    lora_pad = jnp.zeros((HEAD_DIM, RWKV_WIDTH), BF16)
    gates = (row(mu_shift), row(w0), jnp.concatenate([bf(w_lora_up), lora_pad], axis=0), row(a0),
             jnp.concatenate([lora_pad, bf(a_lora_up)], axis=0), bf(g_lora_up), row(k_k), row(k_a), row(r_k))
    ones = jnp.asarray(_head_block_ones(GROUP), BF16)

    def front(x, first, t_seq):
        return _ffn_inproj(x.reshape(-1, D_MODEL), first, ffn1, row(norm_mix), bf(w_in), gates, ones, t_seq)

    tail = (row(ln_x_w), row(ln_x_b), ones, w_out_b[:RWKV_WIDTH], w_out_b[RWKV_WIDTH:], ffn2, norm_final_row)

    x1, pa, r, k, v, kk, beta, lam, g, bonus, plast = front(x_prompt, jnp.zeros((bp, 1, RWKV_COLS), F32), tp)
    pa = pa.reshape(bp, tp, ATT_COLS)
    at = _band_attn(pa, _rel_bias(rel_bias[0], CHUNK), ATT_WINDOW)
    y_prompt, s_bd = _wkv_out_ffn(x1, bonus, g, at.reshape(bp * tp, ATT_WIDTH), r, k, v, kk, beta, lam,
                                  *tail, tp)
    y_prompt = y_prompt.reshape(bp, tp, D_MODEL)
    p_shift = plast.reshape(bp, -1, RWKV_COLS)[:, -1, :][None]
    p_wkv = s_bd[None]
    p_k, p_v = _cache_rows(pa, min(ATT_WINDOW, tp))

    x1, pa, r, k, v, kk, beta, lam, g, bonus, plast = front(x_sample, jnp.repeat(state_shift[0], ts, axis=0), ts)
    seq = lambda z: z.reshape(bs, ts, RWKV_WIDTH)
    y, s_bd = _wkv(seq(r), seq(k), seq(v), seq(kk), seq(beta), seq(lam), state_wkv[0])
    pa = pa.reshape(bs, ts, ATT_COLS)
    q = _heads_first(pa[:, :, :ATT_WIDTH].astype(F32))
    s_k = _heads_first(pa[:, :, ATT_WIDTH:2 * ATT_WIDTH].astype(F32))
    s_v = _heads_first(pa[:, :, 2 * ATT_WIDTH:].astype(F32))
    at = _step_attn(q, cache_attn_k[0], cache_attn_v[0], s_k, s_v, _rel_bias(rel_bias[0], ts))
    at = at.transpose(0, 2, 1, 3).reshape(bs * ts, ATT_WIDTH).astype(BF16)
    y_sample = _out_ffn(x1, y.reshape(bs * ts, RWKV_WIDTH), bonus, g, at, *tail).reshape(bs, ts, D_MODEL)
    s_shift = plast.reshape(bs, ts, RWKV_COLS)[:, -1, :][None]
    s_wkv = s_bd[None]

    return (y_prompt, y_sample, p_shift, p_wkv, p_k[None], p_v[None], s_shift, s_wkv, s_k[None], s_v[None])
```

```python
import functools
import math

import numpy as np
import jax
import jax.numpy as jnp
from jax import lax
from jax.experimental import pallas as pl
from jax.experimental.pallas import tpu as pltpu

F32 = jnp.float32
BF16 = jnp.bfloat16

D_MODEL = 1024
HEAD_DIM = 64
RWKV_WIDTH = 512
ATT_WIDTH = 512
N_HEADS = 8
LORA_COLS = 256
RWKV_COLS = 3 * RWKV_WIDTH + LORA_COLS
ATT_COLS = 3 * ATT_WIDTH
D_FF = 2816
CHUNK = 64
ATT_WINDOW = 512
REL_MAX = 128
NORM_EPS = 1e-5
GN_EPS = 64e-5
NEG_INF = -1e30

GROUP = 256
HEADS_PER_GROUP = GROUP // HEAD_DIM
N_GROUPS = RWKV_WIDTH // GROUP
KEY_PAD = ATT_WINDOW + 2 * CHUNK
LOG2_E = math.log2(math.e)
QK_SCALE = HEAD_DIM ** -0.5 * LOG2_E

TOKEN_TILE = 512
ROW_PART = 256
WKV_BATCH = 4
WKV_TILE = 128
FF_CHUNK = 256
FUSED_BURST = 8
FUSED_CHAIN_GAP = 2
VMEM_LIMIT_BYTES = 56 * 1024 * 1024
FUSED_VMEM_LIMIT_BYTES = 60 * 1024 * 1024


def _rms(x, g):
    ms = jnp.mean(x * x, axis=-1, keepdims=True)
    return x * lax.rsqrt(ms + NORM_EPS) * g


def _dot(a, b):
    return jnp.dot(a, b, preferred_element_type=F32)


def _dot_nt(a, b):
    return lax.dot_general(a, b, (((1,), (1,)), ((), ())), preferred_element_type=F32)


def _row_parts(tm):
    n = max(1, tm // ROW_PART)
    return [slice(i * (tm // n), (i + 1) * (tm // n)) for i in range(n)]


def _head_sums(x, ones_ref):
    return jnp.concatenate([_dot(x[:, g * GROUP:(g + 1) * GROUP], ones_ref[...]) for g in range(N_GROUPS)],
                           axis=1)


def _const_spec(shape):
    return pl.BlockSpec(shape, lambda *_: (0,) * len(shape), pipeline_mode=pl.Buffered(1))


def _head_block_ones(n):
    h = np.arange(n) // HEAD_DIM
    return (h[:, None] == h[None, :]).astype(np.float32)


def _ffn_inproj_kernel(x_ref, first_ref, g1_ref, wg_ref, wu_ref, wd_ref, gm_ref, win_ref,
                       mu_ref, w0_ref, wl_ref, a0_ref, al_ref, gl_ref, kkw_ref, ka_ref, rk_ref, ones_ref,
                       x1_ref, pa_ref, r_ref, k_ref, v_ref, kk_ref, beta_ref, lam_ref, g_ref, bonus_ref,
                       plast_ref, carry_scr, *, tm, t_seq):
    parts = _row_parts(tm)
    rows = parts[0].stop
    q_cols = slice(RWKV_COLS, RWKV_COLS + ATT_WIDTH)

    def gate_up(rs):
        h = _rms(x_ref[rs, :], g1_ref[...]).astype(BF16)
        return _dot(h, wg_ref[...]), _dot(h, wu_ref[...])

    def down_inproj(rs, gate, up):
        act = (gate * jax.nn.sigmoid(gate) * up).astype(BF16)
        x1 = x_ref[rs, :] + 0.5 * _dot(act, wd_ref[...])
        x1_ref[rs, :] = x1
        h2 = _rms(x1, gm_ref[...]).astype(BF16)
        p = _dot(h2, win_ref[:, :RWKV_COLS])
        pa_ref[rs, :ATT_WIDTH] = (_dot(h2, win_ref[:, q_cols]) * QK_SCALE).astype(BF16)
        pa_ref[rs, ATT_WIDTH:] = _dot(h2, win_ref[:, RWKV_COLS + ATT_WIDTH:]).astype(BF16)
        return p

    def gates(rs, p, before):
        row = lax.broadcasted_iota(jnp.int32, (rows, RWKV_COLS), 0)
        first_row = (row == 0) if t_seq >= tm else ((row & (t_seq - 1)) == 0)
        xs = p + (jnp.where(first_row, before, pltpu.roll(p, 1, axis=0)) - p) * mu_ref[...]
        r = xs[:, 0:RWKV_WIDTH]
        k = xs[:, RWKV_WIDTH:2 * RWKV_WIDTH]
        v = xs[:, 2 * RWKV_WIDTH:3 * RWKV_WIDTH]
        lora_in = xs[:, 3 * RWKV_WIDTH:3 * RWKV_WIDTH + 128]
        gate_in = xs[:, 3 * RWKV_WIDTH + 128:]
        z = w0_ref[...] + _dot(jnp.tanh(lora_in).astype(BF16), wl_ref[...])
        a = jax.nn.sigmoid(a0_ref[...] + _dot(lora_in.astype(BF16), al_ref[...]))
        g_ref[rs, :] = _dot(jax.nn.sigmoid(gate_in).astype(BF16), gl_ref[...]).astype(BF16)
        kk = k * kkw_ref[...]
        kk = kk * lax.rsqrt(jnp.maximum(_head_sums((kk * kk).astype(BF16), ones_ref), 1e-24))
        k = k * (1.0 + (a - 1.0) * ka_ref[...])
        lam_ref[rs, :] = -math.exp(-0.5) * jax.nn.sigmoid(z)
        r_ref[rs, :] = r.astype(BF16)
        k_ref[rs, :] = k.astype(BF16)
        v_ref[rs, :] = v.astype(BF16)
        kk_ref[rs, :] = kk.astype(BF16)
        beta_ref[rs, :] = (a * kk).astype(BF16)
        bonus_ref[rs, :] = (_head_sums((r * k * rk_ref[...]).astype(BF16), ones_ref) * v).astype(BF16)

    if t_seq >= tm:
        @pl.when(pl.program_id(0) % (t_seq // tm) == 0)
        def _():
            carry_scr[...] = first_ref[0]
        before = carry_scr[...]
    else:
        assert len(parts) == 1
        before = first_ref[...]

    gu = gate_up(parts[0])
    for i, rs in enumerate(parts):
        p = down_inproj(rs, *gu)
        if i + 1 < len(parts):
            gu = gate_up(parts[i + 1])
        gates(rs, p, before)
        before = p[rows - 1:rows, :]
    if t_seq >= tm:
        carry_scr[...] = before
        plast_ref[0] = before
    else:
        plast_ref[...] = p


def _ffn_inproj(x, first, ffn, gm, win, gates, ones, t_seq):
    n = x.shape[0]
    tm = min(TOKEN_TILE, n)
    assert n % tm == 0 and (t_seq % tm == 0 or tm % t_seq == 0) and t_seq & (t_seq - 1) == 0
    row = lambda i: (i, 0)
    vec = lambda m: _const_spec((1, m))
    wide = lambda dt: jax.ShapeDtypeStruct((n, RWKV_WIDTH), dt)
    wide_spec = pl.BlockSpec((tm, RWKV_WIDTH), row)
    if t_seq >= tm:
        first_spec = pl.BlockSpec((1, 1, RWKV_COLS), lambda i: (i // (t_seq // tm), 0, 0))
        plast_shape = jax.ShapeDtypeStruct((n // tm, 1, RWKV_COLS), F32)
        plast_spec = pl.BlockSpec((1, 1, RWKV_COLS), lambda i: (i, 0, 0))
    else:
        first_spec = pl.BlockSpec((tm, RWKV_COLS), row)
        plast_shape = jax.ShapeDtypeStruct((n, RWKV_COLS), F32)
        plast_spec = pl.BlockSpec((tm, RWKV_COLS), row)
    return pl.pallas_call(
        functools.partial(_ffn_inproj_kernel, tm=tm, t_seq=t_seq),
        grid=(n // tm,),
        in_specs=[pl.BlockSpec((tm, D_MODEL), row), first_spec,
                  vec(D_MODEL), _const_spec((D_MODEL, D_FF)), _const_spec((D_MODEL, D_FF)),
                  _const_spec((D_FF, D_MODEL)), vec(D_MODEL), _const_spec((D_MODEL, RWKV_COLS + ATT_COLS)),
                  vec(RWKV_COLS), vec(RWKV_WIDTH), _const_spec((128, RWKV_WIDTH)), vec(RWKV_WIDTH),
                  _const_spec((128, RWKV_WIDTH)), _const_spec((128, RWKV_WIDTH)),
                  vec(RWKV_WIDTH), vec(RWKV_WIDTH), vec(RWKV_WIDTH), _const_spec((GROUP, GROUP))],
        out_specs=[pl.BlockSpec((tm, D_MODEL), row), pl.BlockSpec((tm, ATT_COLS), row)]
                  + [wide_spec] * 8 + [plast_spec],
        out_shape=[jax.ShapeDtypeStruct((n, D_MODEL), F32), jax.ShapeDtypeStruct((n, ATT_COLS), BF16),
                   wide(BF16), wide(BF16), wide(BF16), wide(BF16), wide(BF16), wide(F32), wide(BF16),
                   wide(BF16), plast_shape],
        scratch_shapes=[pltpu.VMEM((1, RWKV_COLS), F32)],
        compiler_params=pltpu.CompilerParams(dimension_semantics=("arbitrary",),
                                             vmem_limit_bytes=VMEM_LIMIT_BYTES),
        name="ffn_inproj",
    )(x, first, *ffn, gm, win, *gates, ones)


def _out_ffn_kernel(x1_ref, y_ref, bonus_ref, g_ref, at_ref, lnw_ref, lnb_ref, ones_ref, wor_ref, woa_ref,
                    g2_ref, wg_ref, wu_ref, wd_ref, gf_ref, o_ref):
    parts = _row_parts(x1_ref.shape[0])
    nparts = range(len(parts))
    inv_n = 1.0 / HEAD_DIM
    mean = [_head_sums(y_ref[rs, :], ones_ref) * inv_n for rs in parts]
    d = [y_ref[parts[i], :].astype(F32) - mean[i] for i in nparts]
    var = [_head_sums((t * t).astype(BF16), ones_ref) * inv_n for t in d]
    ro = []
    for i in nparts:
        rs = parts[i]
        yn = d[i] * lax.rsqrt(var[i] + GN_EPS) * lnw_ref[...] + lnb_ref[...]
        ro.append(((yn + bonus_ref[rs, :].astype(F32)) * g_ref[rs, :].astype(F32)).astype(BF16))
    x2 = [x1_ref[parts[i], :] + _dot(ro[i], wor_ref[...]) + _dot(at_ref[parts[i], :], woa_ref[...])
          for i in nparts]
    h = [_rms(t, g2_ref[...]).astype(BF16) for t in x2]
    gu = [(_dot(hi, wg_ref[...]), _dot(hi, wu_ref[...])) for hi in h]
    act = [(gate * jax.nn.sigmoid(gate) * up).astype(BF16) for gate, up in gu]
    x3 = [x2[i] + 0.5 * _dot(act[i], wd_ref[...]) for i in nparts]
    for i in nparts:
        o_ref[parts[i], :] = _rms(x3[i], gf_ref[...])


def _out_ffn(x1, y, bonus, g, at, lnw, lnb, ones, wor, woa, ffn, gf):
    n = x1.shape[0]
    tm = min(TOKEN_TILE, n)
    row = lambda i: (i, 0)
    vec = lambda m: _const_spec((1, m))
    wide_spec = pl.BlockSpec((tm, RWKV_WIDTH), row)
    return pl.pallas_call(
        _out_ffn_kernel,
        grid=(n // tm,),
        in_specs=[pl.BlockSpec((tm, D_MODEL), row), wide_spec, wide_spec, wide_spec, wide_spec,
                  vec(RWKV_WIDTH), vec(RWKV_WIDTH), _const_spec((GROUP, GROUP)),
                  _const_spec((RWKV_WIDTH, D_MODEL)), _const_spec((ATT_WIDTH, D_MODEL)),
                  vec(D_MODEL), _const_spec((D_MODEL, D_FF)), _const_spec((D_MODEL, D_FF)),
                  _const_spec((D_FF, D_MODEL)), vec(D_MODEL)],
        out_specs=pl.BlockSpec((tm, D_MODEL), row),
        out_shape=jax.ShapeDtypeStruct((n, D_MODEL), F32),
        compiler_params=pltpu.CompilerParams(dimension_semantics=("arbitrary",),
                                             vmem_limit_bytes=VMEM_LIMIT_BYTES),
        name="out_ffn",
    )(x1, y, bonus, g, at, lnw, lnb, ones, wor, woa, *ffn, gf)


def _bias_kernel(tab_ref, o_ref, *, cq):
    qi = lax.broadcasted_iota(jnp.int32, (cq, KEY_PAD), 0)
    kj = lax.broadcasted_iota(jnp.int32, (cq, KEY_PAD), 1)
    idx = jnp.clip(kj - ATT_WINDOW - qi, -REL_MAX, REL_MAX) + REL_MAX
    pad = kj >= ATT_WINDOW + cq
    for h in range(N_HEADS):
        def body(t, acc):
            return jnp.where(idx == t, tab_ref[h, t], acc)
        acc = lax.fori_loop(0, REL_MAX + cq, body, jnp.zeros((cq, KEY_PAD), F32))
        o_ref[h * cq:(h + 1) * cq, :] = jnp.where(pad, NEG_INF, acc * LOG2_E)


def _rel_bias(table, cq):
    return pl.pallas_call(
        functools.partial(_bias_kernel, cq=cq),
        in_specs=[pl.BlockSpec(memory_space=pltpu.SMEM)],
        out_specs=pl.BlockSpec(memory_space=pltpu.VMEM),
        out_shape=jax.ShapeDtypeStruct((N_HEADS * cq, KEY_PAD), F32),
        name=f"rel_bias_{cq}",
    )(table)


def _head_masks(rows, dtype):
    lane_head = lax.broadcasted_iota(jnp.int32, (rows, GROUP), 1) >> 6
    return [(lane_head == h).astype(F32).astype(dtype) for h in range(HEADS_PER_GROUP)]


def _block_diag_rows(x, masks):
    return jnp.concatenate([x * m for m in masks], axis=0)


def _band_attn_kernel(q_ref, kp_ref, kc_ref, vp_ref, vc_ref, bias_ref, o_ref, kbuf, vbuf, *, tq):
    kbuf[0:tq, :] = kp_ref[0]
    kbuf[tq:2 * tq, :] = kc_ref[0]
    kbuf[2 * tq:, :] = jnp.zeros((CHUNK, ATT_WIDTH), BF16)
    vbuf[0:tq, :] = vp_ref[0]
    vbuf[tq:2 * tq, :] = vc_ref[0]
    vbuf[2 * tq:, :] = jnp.zeros((CHUNK, ATT_WIDTH), BF16)
    masks_b = _head_masks(CHUNK, BF16)
    masks_f = _head_masks(CHUNK, F32)
    rows = HEADS_PER_GROUP * CHUNK
    cells = [(j, g) for j in range(tq // CHUNK) for g in range(N_GROUPS)]
    lanes = [slice(g * GROUP, (g + 1) * GROUP) for g in range(N_GROUPS)]
    win = [slice(j * CHUNK + tq - ATT_WINDOW, j * CHUNK + tq - ATT_WINDOW + KEY_PAD)
           for j in range(tq // CHUNK)]

    def run(first_tile):
        def scores(j, g):
            lhs = _block_diag_rows(q_ref[0, j * CHUNK:(j + 1) * CHUNK, lanes[g]], masks_b)
            return _dot_nt(lhs, kbuf[win[j], lanes[g]])

        def softmax(j, g, s):
            s = s + bias_ref[g * rows:(g + 1) * rows, :]
            if first_tile:
                kj = lax.broadcasted_iota(jnp.int32, (rows, KEY_PAD), 1)
                s = jnp.where(kj >= tq - j * CHUNK, s, NEG_INF)
            p = jnp.exp2(s - jnp.max(s, axis=-1, keepdims=True))
            return p.astype(BF16), 1.0 / jnp.sum(p, axis=-1, keepdims=True)

        def values(j, g, p, inv):
            o_full = _dot(p, vbuf[win[j], lanes[g]]) * inv
            o = o_full[0:CHUNK] * masks_f[0]
            for h in range(1, HEADS_PER_GROUP):
                o = o + o_full[h * CHUNK:(h + 1) * CHUNK] * masks_f[h]
            return o

        s, pz, o = {}, {}, {}
        for n in range(len(cells) + 2):
            if n < len(cells):
                s[n] = scores(*cells[n])
            if 1 <= n <= len(cells):
                pz[n - 1] = softmax(*cells[n - 1], s.pop(n - 1))
            if n >= 2:
                j, g = cells[n - 2]
                o[g] = values(j, g, *pz.pop(n - 2))
                if g == N_GROUPS - 1:
                    o_ref[0, j * CHUNK:(j + 1) * CHUNK, :] = jnp.concatenate(
                        [o[gg] for gg in range(N_GROUPS)], axis=1).astype(BF16)

    @pl.when(pl.program_id(1) == 0)
    def _():
        run(True)

    @pl.when(pl.program_id(1) > 0)
    def _():
        run(False)


def _band_attn(p_att, bias, tq):
    b, t, _ = p_att.shape
    assert tq == ATT_WINDOW and t % tq == 0
    cur = lambda col: (lambda bi, i: (bi, i, col))
    prev = lambda col: (lambda bi, i: (bi, jnp.maximum(i - 1, 0), col))
    blk = (1, tq, ATT_WIDTH)
    return pl.pallas_call(
        functools.partial(_band_attn_kernel, tq=tq),
        grid=(b, t // tq),
        in_specs=[pl.BlockSpec(blk, cur(0)), pl.BlockSpec(blk, prev(1)), pl.BlockSpec(blk, cur(1)),
                  pl.BlockSpec(blk, prev(2)), pl.BlockSpec(blk, cur(2)),
                  _const_spec((N_HEADS * CHUNK, KEY_PAD))],
        out_specs=pl.BlockSpec(blk, cur(0)),
        out_shape=jax.ShapeDtypeStruct((b, t, ATT_WIDTH), BF16),
        scratch_shapes=[pltpu.VMEM((2 * tq + CHUNK, ATT_WIDTH), BF16),
                        pltpu.VMEM((2 * tq + CHUNK, ATT_WIDTH), BF16)],
        compiler_params=pltpu.CompilerParams(dimension_semantics=("arbitrary", "arbitrary"),
                                             vmem_limit_bytes=VMEM_LIMIT_BYTES),
        name="band_attn",
    )(p_att, p_att, p_att, p_att, p_att, bias)


def _step_attn_kernel(q_ref, kc_ref, vc_ref, kn_ref, vn_ref, bias_ref, o_ref, *, tn):
    heads = range(N_HEADS)
    q = [q_ref[0, h].astype(BF16) for h in heads]
    s1 = [_dot_nt(q[h], kc_ref[0, 0, h].astype(BF16)) for h in heads]
    s2 = [_dot_nt(q[h], kn_ref[0, h].astype(BF16)) for h in heads]
    p1, p2, inv = [], [], []
    for h in heads:
        b = bias_ref[h * tn:(h + 1) * tn, :]
        a1 = s1[h] + b[:, :ATT_WINDOW]
        a2 = s2[h] + b[:, ATT_WINDOW:ATT_WINDOW + tn]
        m = jnp.maximum(jnp.max(a1, axis=-1, keepdims=True), jnp.max(a2, axis=-1, keepdims=True))
        e1 = jnp.exp2(a1 - m)
        e2 = jnp.exp2(a2 - m)
        inv.append(1.0 / (jnp.sum(e1, axis=-1, keepdims=True) + jnp.sum(e2, axis=-1, keepdims=True)))
        p1.append(e1.astype(BF16))
        p2.append(e2.astype(BF16))
    o1 = [_dot(p1[h], vc_ref[0, 0, h].astype(BF16)) for h in heads]
    o2 = [_dot(p2[h], vn_ref[0, h].astype(BF16)) for h in heads]
    for h in heads:
        o_ref[0, h] = (o1[h] + o2[h]) * inv[h]


def _step_attn(q, k_cache, v_cache, k_new, v_new, bias):
    b, _, tn, _ = q.shape
    new = pl.BlockSpec((1, N_HEADS, tn, HEAD_DIM), lambda i: (i, 0, 0, 0))
    old = pl.BlockSpec((1, 1, N_HEADS, ATT_WINDOW, HEAD_DIM), lambda i: (0, i, 0, 0, 0))
    return pl.pallas_call(
        functools.partial(_step_attn_kernel, tn=tn),
        grid=(b,),
        in_specs=[new, old, old, new, new, _const_spec((N_HEADS * tn, KEY_PAD))],
        out_specs=new,
        out_shape=jax.ShapeDtypeStruct((b, N_HEADS, tn, HEAD_DIM), F32),
        compiler_params=pltpu.CompilerParams(dimension_semantics=("arbitrary",)),
        name="step_attn",
    )(q, k_cache, v_cache, k_new, v_new, bias)


def _split3(x):
    hi = x.astype(BF16)
    r1 = x - hi.astype(F32)
    mid = r1.astype(BF16)
    lo = (r1 - mid.astype(F32)).astype(BF16)
    return hi, mid, lo


def _wkv_stages(load, state, tri3, hm, bdm, eye4, lanes_of, n_chunks, burst=None, chain_gap=1):
    c = CHUNK
    burst = burst or len(lanes_of) * n_chunks
    t_row = lax.broadcasted_iota(jnp.int32, (c, GROUP), 0)
    s_col = lax.broadcasted_iota(jnp.int32, (c, GROUP), 1) & (c - 1)
    strict = s_col < t_row
    incl = s_col <= t_row
    zero = jnp.zeros((), F32)

    def bd(x):
        return jnp.concatenate([x] * HEADS_PER_GROUP, axis=0) * hm

    cells = [(lane, ci) for ci in range(n_chunks) for lane in lanes_of]

    def each(fn):
        out = {}
        for n, z in enumerate(cells):
            out[z] = fn(z)
            if (n + 1) % burst == 0:
                yield 1
        if len(cells) % burst:
            yield 1
        return out

    lam = {z: load("lam", *z) for z in cells}
    cum = yield from each(lambda z: _dot(tri3, jnp.concatenate(_split3(lam[z]), axis=0)))
    tot = {z: cum[z][c - 1:c, :] for z in cells}
    lhs, bt, kt, bk, v_bd = {}, {}, {}, {}, {}
    for z in cells:
        r, k, kk, beta = (load(name, *z).astype(F32) for name in ("r", "k", "kk", "beta"))
        e_neg = jnp.exp(-cum[z])
        e_rem = jnp.exp(tot[z] - cum[z])
        lhs[z] = jnp.concatenate([-kk * jnp.exp(cum[z] - lam[z]), r * jnp.exp(cum[z])],
                                 axis=0).astype(BF16)
        bt[z] = bd((beta * e_neg).astype(BF16))
        kt[z] = bd((k * e_neg).astype(BF16))
        bk[z] = jnp.concatenate([beta * e_rem, k * e_rem], axis=0).astype(BF16)
        v_bd[z] = bd(load("v", *z))
    ab = yield from each(lambda z: _dot_nt(lhs[z], bt[z]))
    ak = yield from each(lambda z: _dot_nt(lhs[z], kt[z]))
    a_rb = {z: jnp.where(incl, ab[z][c:], zero).astype(BF16) for z in cells}
    a_ak = {z: jnp.where(strict, ak[z][:c], zero).astype(BF16) for z in cells}
    a_rk = {z: jnp.where(incl, ak[z][c:], zero).astype(BF16) for z in cells}
    n_k = {z: jnp.where(strict, ab[z][:c], zero) for z in cells}
    t_m = {z: eye4 + n_k[z] for z in cells}
    n_b = {z: n_k[z].astype(BF16) for z in cells}
    n_b = yield from each(lambda z: _dot(n_b[z], bd(n_b[z])).astype(BF16))
    for _ in range(int(math.log2(c)) - 2):
        both = yield from each(
            lambda z: _dot(jnp.concatenate([t_m[z].astype(BF16), n_b[z]], axis=0), bd(n_b[z])))
        t_m = {z: t_m[z] + both[z][:c] for z in cells}
        n_b = {z: both[z][c:].astype(BF16) for z in cells}
    t_b = yield from each(lambda z: (t_m[z] + _dot(t_m[z].astype(BF16), bd(n_b[z]))).astype(BF16))
    wy_v = yield from each(lambda z: _dot(jnp.concatenate([a_ak[z], a_rk[z]], axis=0), v_bd[z]))
    w_v = {z: wy_v[z][:c] for z in cells}
    y_v = {z: wy_v[z][c:] for z in cells}

    def outputs(zs, x, u):
        return [(z[0], z[1], x[z][c:] + y_v[z] + _dot(a_rb[z], bd(u[z].astype(BF16)))) for z in zs]

    pending = None
    for ci in range(n_chunks):
        zs = [(lane, ci) for lane in lanes_of]
        x = {z: _dot_nt(lhs[z], state[z[0]].astype(BF16)) for z in zs}
        if pending is not None:
            yield from outputs(*pending)
        yield chain_gap
        u = {z: _dot(t_b[z], bd((x[z][:c] + w_v[z]).astype(BF16))) for z in zs}
        yield chain_gap
        for z in zs:
            uv_t = jnp.concatenate([u[z], load("v", *z).astype(F32)], axis=0).T.astype(BF16)
            state[z[0]] = state[z[0]] * jnp.exp(tot[z]) + _dot(uv_t, bk[z]) * bdm
        yield chain_gap
        pending = (zs, x, u)
    yield from outputs(*pending)


def _store_head_states(out_ref, out_idx, s_scr, scr_idx):
    for g in range(N_GROUPS):
        s = s_scr[g] if scr_idx is None else s_scr[scr_idx, g]
        for h in range(HEADS_PER_GROUP):
            blk = slice(h * HEAD_DIM, (h + 1) * HEAD_DIM)
            out_ref[out_idx, g * HEADS_PER_GROUP + h] = s[blk, blk]


def _wkv_kernel(r_ref, k_ref, v_ref, kk_ref, beta_ref, lam_ref, s0_ref, tri3_ref, hm_ref, bdm_ref, eye4_ref,
                y_ref, s_out_ref, s_scr, *, nb, tt):
    ti = pl.program_id(1)

    @pl.when(ti == 0)
    def _():
        spread = eye4_ref[...].astype(BF16)
        for b in range(nb):
            for g in range(N_GROUPS):
                heads = jnp.concatenate([s0_ref[0, b, g * HEADS_PER_GROUP + h] for h in range(HEADS_PER_GROUP)],
                                        axis=0)
                hi, mid, lo = _split3(heads)
                s_scr[b, g] = (_dot(hi, spread) + _dot(mid, spread) + _dot(lo, spread)) * bdm_ref[...]

    c = CHUNK
    n_chunks = -(-tt // c)
    pad = n_chunks * c - tt
    refs = dict(r=r_ref, k=k_ref, v=v_ref, kk=kk_ref, beta=beta_ref, lam=lam_ref)

    def load(name, lane, ci):
        b, g = lane
        lanes = slice(g * GROUP, (g + 1) * GROUP)
        if pad == 0:
            return refs[name][b, ci * c:(ci + 1) * c, lanes]
        x = refs[name][b, :, lanes]
        return jnp.concatenate([x, jnp.zeros((pad, GROUP), x.dtype)], axis=0)

    lanes_of = [(b, g) for b in range(nb) for g in range(N_GROUPS)]
    state = {lane: s_scr[lane[0], lane[1]] for lane in lanes_of}
    y = {}
    for out in _wkv_stages(load, state, tri3_ref[...], hm_ref[...], bdm_ref[...], eye4_ref[...],
                           lanes_of, n_chunks):
        if not isinstance(out, int):
            lane, ci, y_c = out
            y[(lane, ci)] = y_c
    for lane in lanes_of:
        s_scr[lane[0], lane[1]] = state[lane]
    for b in range(nb):
        rows = [jnp.concatenate([y[((b, g), ci)] for g in range(N_GROUPS)], axis=1) for ci in range(n_chunks)]
        y_b = rows[0] if n_chunks == 1 else jnp.concatenate(rows, axis=0)
        y_ref[b] = y_b[:tt].astype(BF16)

    @pl.when(ti == pl.num_programs(1) - 1)
    def _():
        for b in range(nb):
            _store_head_states(s_out_ref, b, s_scr, b)


def _alternate(main, side):
    results = []
    for out in side:
        if isinstance(out, int):
            for _ in range(out):
                next(main, _DONE)
        else:
            results.append(out)
    for _ in main:
        pass
    return results


_DONE = object()


def _wkv_out_ffn_kernel(x1_ref, bonus_ref, g_ref, at_ref, r_ref, k_ref, v_ref, kk_ref, beta_ref, lam_ref,
                        lnw_ref, lnb_ref, ones_ref, wor_ref, woa_ref, g2_ref, wg_ref, wu_ref, wd_ref, gf_ref,
                        tri3_ref, bdm_ref, eye4_ref, o_ref, s_out_ref, y_scr, s_scr, *, tm, tiles_per_seq):
    i = pl.program_id(0)
    n_tiles = pl.num_programs(0) - 1
    tile = jnp.minimum(i, n_tiles - 1)

    @pl.when(i == 0)
    def _():
        y_scr[...] = jnp.zeros_like(y_scr)

    @pl.when(tile % tiles_per_seq == 0)
    def _():
        s_scr[...] = jnp.zeros_like(s_scr)

    parts = _row_parts(tm)
    inv_n = 1.0 / HEAD_DIM
    y_prev = [y_scr[rs, :] for rs in parts]
    ff_cols = [slice(j, min(j + FF_CHUNK, D_FF)) for j in range(0, D_FF, FF_CHUNK)]

    def ffn_stages():
        x2, h = [], []
        for i_p, rs in enumerate(parts):
            mean = _head_sums(y_prev[i_p], ones_ref) * inv_n
            yield
            d = y_prev[i_p].astype(F32) - mean
            var = _head_sums((d * d).astype(BF16), ones_ref) * inv_n
            yield
            yn = d * lax.rsqrt(var + GN_EPS) * lnw_ref[...] + lnb_ref[...]
            ro = ((yn + bonus_ref[rs, :].astype(F32)) * g_ref[rs, :].astype(F32)).astype(BF16)
            x2.append(x1_ref[rs, :] + _dot(ro, wor_ref[...]) + _dot(at_ref[rs, :], woa_ref[...]))
            yield
            h.append(_rms(x2[i_p], g2_ref[...]).astype(BF16))
        acc = [None] * len(parts)
        for cols in ff_cols:
            gu = []
            for i_p in range(len(parts)):
                gate = _dot(h[i_p], wg_ref[:, cols])
                yield
                up = _dot(h[i_p], wu_ref[:, cols])
                yield
                gu.append((gate, up))
            for i_p, (gate, up) in enumerate(gu):
                act = (gate * jax.nn.sigmoid(gate) * up).astype(BF16)
                down = _dot(act, wd_ref[cols, :])
                acc[i_p] = down if acc[i_p] is None else acc[i_p] + down
                yield
        for i_p, rs in enumerate(parts):
            o_ref[rs, :] = _rms(x2[i_p] + 0.5 * acc[i_p], gf_ref[...])

    c = CHUNK
    refs = dict(r=r_ref, k=k_ref, v=v_ref, kk=kk_ref, beta=beta_ref, lam=lam_ref)

    def load(name, g, ci):
        return refs[name][ci * c:(ci + 1) * c, g * GROUP:(g + 1) * GROUP]

    lanes_of = list(range(N_GROUPS))
    state = {g: s_scr[g] for g in lanes_of}
    wkv = _wkv_stages(load, state, tri3_ref[...], ones_ref[...], bdm_ref[...], eye4_ref[...], lanes_of, tm // c,
                      burst=FUSED_BURST, chain_gap=FUSED_CHAIN_GAP)
    for g, ci, y_c in _alternate(ffn_stages(), wkv):
        y_scr[ci * c:(ci + 1) * c, g * GROUP:(g + 1) * GROUP] = y_c.astype(BF16)
    for g in lanes_of:
        s_scr[g] = state[g]

    @pl.when(i < n_tiles)
    def _():
        _store_head_states(s_out_ref, 0, s_scr, None)


def _wkv_out_ffn(x1, bonus, g, at, r, k, v, kk, beta, lam, lnw, lnb, ones, wor, woa, ffn, gf, t_seq):
    n = x1.shape[0]
    tm = TOKEN_TILE
    assert n % tm == 0 and t_seq % tm == 0
    n_tiles = n // tm
    c = CHUNK
    tri = np.tril(np.ones((c, c), np.float32))
    tri3 = jnp.asarray(np.concatenate([tri, tri, tri], axis=1), BF16)
    bdm = jnp.asarray(_head_block_ones(GROUP))
    eye4 = jnp.asarray(np.tile(np.eye(c, dtype=np.float32), (1, HEADS_PER_GROUP)))
    prev = lambda i: (jnp.maximum(i - 1, 0), 0)
    cur = lambda i: (jnp.minimum(i, n_tiles - 1), 0)
    vec = lambda m: _const_spec((1, m))
    wide_prev = pl.BlockSpec((tm, RWKV_WIDTH), prev)
    wide_cur = pl.BlockSpec((tm, RWKV_WIDTH), cur)
    state = pl.BlockSpec((1, N_HEADS, HEAD_DIM, HEAD_DIM),
                         lambda i: (jnp.minimum(i, n_tiles - 1) // (t_seq // tm), 0, 0, 0))
    return pl.pallas_call(
        functools.partial(_wkv_out_ffn_kernel, tm=tm, tiles_per_seq=t_seq // tm),
        grid=(n_tiles + 1,),
        in_specs=[pl.BlockSpec((tm, D_MODEL), prev), wide_prev, wide_prev, wide_prev,
                  wide_cur, wide_cur, wide_cur, wide_cur, wide_cur, wide_cur,
                  vec(RWKV_WIDTH), vec(RWKV_WIDTH), _const_spec((GROUP, GROUP)),
                  _const_spec((RWKV_WIDTH, D_MODEL)), _const_spec((ATT_WIDTH, D_MODEL)),
                  vec(D_MODEL), _const_spec((D_MODEL, D_FF)), _const_spec((D_MODEL, D_FF)),
                  _const_spec((D_FF, D_MODEL)), vec(D_MODEL),
                  _const_spec((c, 3 * c)), _const_spec((GROUP, GROUP)), _const_spec((c, GROUP))],
        out_specs=[pl.BlockSpec((tm, D_MODEL), prev), state],
        out_shape=[jax.ShapeDtypeStruct((n, D_MODEL), F32),
                   jax.ShapeDtypeStruct((n // t_seq, N_HEADS, HEAD_DIM, HEAD_DIM), F32)],
        scratch_shapes=[pltpu.VMEM((tm, RWKV_WIDTH), BF16), pltpu.VMEM((N_GROUPS, GROUP, GROUP), F32)],
        compiler_params=pltpu.CompilerParams(dimension_semantics=("arbitrary",),
                                             vmem_limit_bytes=FUSED_VMEM_LIMIT_BYTES),
        name="wkv_out_ffn",
    )(x1, bonus, g, at, r, k, v, kk, beta, lam, lnw, lnb, ones, wor, woa, *ffn, gf, tri3, bdm, eye4)


def _wkv(r, k, v, kk, beta, lam, s0):
    b, t, _ = r.shape
    nb = WKV_BATCH if b % WKV_BATCH == 0 else 1
    tt = min(WKV_TILE, t)
    assert t % tt == 0
    c = CHUNK
    tri = np.tril(np.ones((c, c), np.float32))
    tri3 = jnp.asarray(np.concatenate([tri, tri, tri], axis=1), BF16)
    ones = _head_block_ones(GROUP)
    eye4 = jnp.asarray(np.tile(np.eye(c, dtype=np.float32), (1, HEADS_PER_GROUP)))
    tok = pl.BlockSpec((nb, tt, RWKV_WIDTH), lambda bi, i: (bi, i, 0))
    state = pl.BlockSpec((nb, N_HEADS, HEAD_DIM, HEAD_DIM), lambda bi, i: (bi, 0, 0, 0))
    return pl.pallas_call(
        functools.partial(_wkv_kernel, nb=nb, tt=tt),
        grid=(b // nb, t // tt),
        in_specs=[tok, tok, tok, tok, tok, tok,
                  pl.BlockSpec((1, nb, N_HEADS, HEAD_DIM, HEAD_DIM), lambda bi, i: (0, bi, 0, 0, 0)),
                  _const_spec((c, 3 * c)), _const_spec((GROUP, GROUP)), _const_spec((GROUP, GROUP)),
                  _const_spec((c, GROUP))],
        out_specs=[tok, state],
        out_shape=[jax.ShapeDtypeStruct((b, t, RWKV_WIDTH), BF16),
                   jax.ShapeDtypeStruct((b, N_HEADS, HEAD_DIM, HEAD_DIM), F32)],
        scratch_shapes=[pltpu.VMEM((nb, N_GROUPS, GROUP, GROUP), F32)],
        compiler_params=pltpu.CompilerParams(dimension_semantics=("arbitrary", "arbitrary"),
                                             vmem_limit_bytes=VMEM_LIMIT_BYTES),
        name="wkv",
    )(r, k, v, kk, beta, lam, s0, tri3, jnp.asarray(ones, BF16), jnp.asarray(ones), eye4)


def _cache_rows_kernel(k_ref, v_ref, ko_ref, vo_ref):
    for h in range(N_HEADS):
        ko_ref[0, h] = k_ref[0, :, h * HEAD_DIM:(h + 1) * HEAD_DIM].astype(F32)
        vo_ref[0, h] = v_ref[0, :, h * HEAD_DIM:(h + 1) * HEAD_DIM].astype(F32)


def _cache_rows(p_att, rows):
    b, t, _ = p_att.shape
    assert t % rows == 0
    last = t // rows - 1
    out = jax.ShapeDtypeStruct((b, N_HEADS, rows, HEAD_DIM), F32)
    out_spec = pl.BlockSpec((1, N_HEADS, rows, HEAD_DIM), lambda i: (i, 0, 0, 0))
    return pl.pallas_call(
        _cache_rows_kernel,
        grid=(b,),
        in_specs=[pl.BlockSpec((1, rows, ATT_WIDTH), lambda i: (i, last, 1)),
                  pl.BlockSpec((1, rows, ATT_WIDTH), lambda i: (i, last, 2))],
        out_specs=[out_spec, out_spec],
        out_shape=[out, out],
        compiler_params=pltpu.CompilerParams(dimension_semantics=("arbitrary",)),
        name="cache_rows",
    )(p_att, p_att)


def _heads_first(x):
    b, t, _ = x.shape
    return x.reshape(b, t, N_HEADS, HEAD_DIM).transpose(0, 2, 1, 3)


def kernel(x_prompt, x_sample, state_shift, state_wkv, cache_attn_k, cache_attn_v, norm_ff1, w_ff1_gate, w_ff1_up, w_ff1_down, norm_mix, w_in, mu_shift, w0, w_lora_up, a0, a_lora_up, g_lora_up, k_k, k_a, r_k, ln_x_w, ln_x_b, rel_bias, w_out, norm_ff2, w_ff2_gate, w_ff2_up, w_ff2_down, norm_final):
    assert norm_ff1.shape[0] == 1, "single layer"
    bp, tp, _ = x_prompt.shape
    bs, ts, _ = x_sample.shape
    assert tp % ATT_WINDOW == 0 and ts <= CHUNK and cache_attn_k.shape[3] == ATT_WINDOW

    row = lambda w: w[0].reshape(1, -1).astype(F32)
    bf = lambda w: w[0].astype(BF16)
    ffn1 = (row(norm_ff1), bf(w_ff1_gate), bf(w_ff1_up), bf(w_ff1_down))
    ffn2 = (row(norm_ff2), bf(w_ff2_gate), bf(w_ff2_up), bf(w_ff2_down))
    w_out_b = bf(w_out)
    norm_final_row = norm_final.reshape(1, D_MODEL).astype(F32)
    lora_pad = jnp.zeros((HEAD_DIM, RWKV_WIDTH), BF16)
    gates = (row(mu_shift), row(w0), jnp.concatenate([bf(w_lora_up), lora_pad], axis=0), row(a0),
             jnp.concatenate([lora_pad, bf(a_lora_up)], axis=0), bf(g_lora_up), row(k_k), row(k_a), row(r_k))
    ones = jnp.asarray(_head_block_ones(GROUP), BF16)

    def front(x, first, t_seq):
        return _ffn_inproj(x.reshape(-1, D_MODEL), first, ffn1, row(norm_mix), bf(w_in), gates, ones, t_seq)

    tail = (row(ln_x_w), row(ln_x_b), ones, w_out_b[:RWKV_WIDTH], w_out_b[RWKV_WIDTH:], ffn2, norm_final_row)

    x1, pa, r, k, v, kk, beta, lam, g, bonus, plast = front(x_prompt, jnp.zeros((bp, 1, RWKV_COLS), F32), tp)
    pa = pa.reshape(bp, tp, ATT_COLS)
    at = _band_attn(pa, _rel_bias(rel_bias[0], CHUNK), ATT_WINDOW)
    y_prompt, s_bd = _wkv_out_ffn(x1, bonus, g, at.reshape(bp * tp, ATT_WIDTH), r, k, v, kk, beta, lam,
                                  *tail, tp)
    y_prompt = y_prompt.reshape(bp, tp, D_MODEL)
    p_shift = plast.reshape(bp, -1, RWKV_COLS)[:, -1, :][None]
    p_wkv = s_bd[None]
    p_k, p_v = _cache_rows(pa, min(ATT_WINDOW, tp))

    x1, pa, r, k, v, kk, beta, lam, g, bonus, plast = front(x_sample, jnp.repeat(state_shift[0], ts, axis=0), ts)
    seq = lambda z: z.reshape(bs, ts, RWKV_WIDTH)
    y, s_bd = _wkv(seq(r), seq(k), seq(v), seq(kk), seq(beta), seq(lam), state_wkv)
    pa = pa.reshape(bs, ts, ATT_COLS)
    q = _heads_first(pa[:, :, :ATT_WIDTH].astype(F32))
    s_k = _heads_first(pa[:, :, ATT_WIDTH:2 * ATT_WIDTH].astype(F32))
    s_v = _heads_first(pa[:, :, 2 * ATT_WIDTH:].astype(F32))
    at = _step_attn(q, cache_attn_k, cache_attn_v, s_k, s_v, _rel_bias(rel_bias[0], ts))
    at = at.transpose(0, 2, 1, 3).reshape(bs * ts, ATT_WIDTH).astype(BF16)
    y_sample = _out_ffn(x1, y.reshape(bs * ts, RWKV_WIDTH), bonus, g, at, *tail).reshape(bs, ts, D_MODEL)
    s_shift = plast.reshape(bs, ts, RWKV_COLS)[:, -1, :][None]
    s_wkv = s_bd[None]

    return (y_prompt, y_sample, p_shift, p_wkv, p_k[None], p_v[None], s_shift, s_wkv, s_k[None], s_v[None])
```
